```python
import math
import jax, jax.numpy as jnp
from jax import lax
import numpy as np

D_MODEL = 1024
BATCH = 2
SEQ = 8192
DEPTH = 1

RET_HEADS = 4
RET_HEAD_DIM = 128
RET_WIDTH = RET_HEADS * RET_HEAD_DIM
RET_CHUNK = 128
SWA_Q_HEADS = 8
SWA_KV_HEADS = 2
SWA_HEAD_DIM = 64
SWA_WIDTH = SWA_Q_HEADS * SWA_HEAD_DIM
SWA_KV_WIDTH = SWA_KV_HEADS * SWA_HEAD_DIM
WINDOW = 128
MIX_WIDTH = RET_WIDTH + SWA_WIDTH
IN_COLS = 4 * RET_WIDTH + SWA_WIDTH + 2 * SWA_KV_WIDTH
ROPE_THETA = 10000.0
PEER_HEADS = 8
PEER_N_KEYS = 128
PEER_N_EXPERTS = PEER_N_KEYS * PEER_N_KEYS
PEER_QUERY_DIM = 256
PEER_HALF = PEER_QUERY_DIM // 2
PEER_TOPK = 16
PEER_TOKEN_BLOCK = 128
LN_EPS = 1e-5
GN_EPS = 1e-6
ALPHA = (2.0 * DEPTH) ** 0.25
BETA = (8.0 * DEPTH) ** -0.25
NEG_INF = -1e30

kernel_name = "hymba_retnet_swa_peer_deepnorm"


def layer_norm(x, w, b):
    xf = x.astype(jnp.float32)
    mu = jnp.mean(xf, axis=-1, keepdims=True)
    var = jnp.mean(jnp.square(xf - mu), axis=-1, keepdims=True)
    return ((xf - mu) * lax.rsqrt(var + LN_EPS) * w.astype(jnp.float32) + b.astype(jnp.float32)).astype(x.dtype)


def head_group_norm(y):
    yf = y.astype(jnp.float32)
    mu = jnp.mean(yf, axis=-1, keepdims=True)
    var = jnp.mean(jnp.square(yf - mu), axis=-1, keepdims=True)
    return ((yf - mu) * lax.rsqrt(var + GN_EPS)).astype(y.dtype)


def rotary(x, pos):
    d = x.shape[-1]
    inv = 1.0 / (ROPE_THETA ** (jnp.arange(0, d, 2, dtype=jnp.float32) / d))
    ang = pos.astype(jnp.float32)[:, None] * inv[None, :]
    cos = jnp.cos(ang)[None, :, None, :]
    sin = jnp.sin(ang)[None, :, None, :]
    xf = x.astype(jnp.float32)
    x1, x2 = xf[..., : d // 2], xf[..., d // 2:]
    out = jnp.concatenate([x1 * cos - x2 * sin, x2 * cos + x1 * sin], axis=-1)
    return out.astype(x.dtype)


def retention(q, k, v):
    B, S, H, d = q.shape
    C = RET_CHUNK
    n = S // C
    dt = q.dtype
    log_g = jnp.log(1.0 - 2.0 ** (-5.0 - jnp.arange(H, dtype=jnp.float32)))
    k = k * (d ** -0.5)
    qc = q.reshape(B, n, C, H, d).transpose(0, 3, 1, 2, 4)
    kc = k.reshape(B, n, C, H, d).transpose(0, 3, 1, 2, 4)
    vc = v.reshape(B, n, C, H, d).transpose(0, 3, 1, 2, 4)
    idx = jnp.arange(C, dtype=jnp.float32)
    diff = idx[:, None] - idx[None, :]
    decay = jnp.where(diff[None] >= 0,
                      jnp.exp(jnp.maximum(diff, 0.0)[None] * log_g[:, None, None]), 0.0).astype(dt)
    zeta = jnp.exp((C - 1.0 - idx)[None] * log_g[:, None]).astype(dt)
    xi = jnp.exp((idx + 1.0)[None] * log_g[:, None]).astype(dt)
    g_chunk = jnp.exp(C * log_g)
    scores = jnp.einsum('bhncd,bhnmd->bhncm', qc, kc) * decay[None, :, None]
    y_intra = jnp.einsum('bhncm,bhnme->bhnce', scores, vc)
    chunk_kv = jnp.einsum('bhnmd,bhnme->bhnde', kc * zeta[None, :, None, :, None], vc)
    kv_seq = jnp.moveaxis(chunk_kv, 2, 0)
    gc = g_chunk.astype(kv_seq.dtype)[None, :, None, None]

    def step(state, kv):
        return state * gc + kv, state

    _, prev = lax.scan(step, jnp.zeros_like(kv_seq[0]), kv_seq)
    prev = jnp.moveaxis(prev, 0, 2)
    y_cross = jnp.einsum('bhncd,bhnde->bhnce', qc * xi[None, :, None, :, None], prev)
    y = (y_intra + y_cross).transpose(0, 2, 3, 1, 4).reshape(B, S, H, d)
    return y


def sliding_window_attention(q, k, v, sinks):
    B, S, Hq, d = q.shape
    Hkv = k.shape[2]
    G = Hq // Hkv
    W = WINDOW
    n = S // W
    qb = q.reshape(B, n, W, Hkv, G, d)
    kb = k.reshape(B, n, W, Hkv, d)
    vb = v.reshape(B, n, W, Hkv, d)
    zk = jnp.zeros_like(kb[:, :1])
    kk = jnp.concatenate([jnp.concatenate([zk, kb[:, :-1]], axis=1), kb], axis=2)
    vv = jnp.concatenate([jnp.concatenate([zk, vb[:, :-1]], axis=1), vb], axis=2)
    s = jnp.einsum('bnqhgd,bnkhd->bnhgqk', qb, kk).astype(jnp.float32) * (d ** -0.5)
    qi = jnp.arange(W)[:, None] + W
    kj = jnp.arange(2 * W)[None, :]
    rel = qi - kj
    blk = jnp.arange(n)
    valid = ((rel >= 0) & (rel < W))[None] & ((blk[:, None, None] > 0) | (kj[None] >= W))
    s = jnp.where(valid[None, :, None, None], s, NEG_INF)
    sink = jnp.broadcast_to(sinks.astype(jnp.float32).reshape(Hkv, G)[None, None, :, :, None, None],
                            s.shape[:-1] + (1,))
    p = jax.nn.softmax(jnp.concatenate([s, sink], axis=-1), axis=-1)[..., :-1]
    o = jnp.einsum('bnhgqk,bnkhd->bnqhgd', p.astype(v.dtype), vv)
    return o.reshape(B, S, Hq, d)


def peer_ffn(x, w_pq, sub_keys, u_tab, v_tab):
    B, S, D = x.shape
    T = B * S
    H, K = PEER_HEADS, PEER_TOPK
    xt = x.reshape(T, D)
    q = (xt @ w_pq).reshape(T, H, 2, PEER_HALF)
    s = jnp.einsum('thpd,hpkd->thpk', q, sub_keys)
    s1, i1 = lax.top_k(s[:, :, 0], K)
    s2, i2 = lax.top_k(s[:, :, 1], K)
    cand = (s1[..., :, None] + s2[..., None, :]).reshape(T, H, K * K)
    cand_idx = (i1[..., :, None] * PEER_N_KEYS + i2[..., None, :]).reshape(T, H, K * K)
    top_s, pos = lax.top_k(cand, K)
    idx = jnp.take_along_axis(cand_idx, pos, axis=-1)
    g = jax.nn.softmax(top_s.astype(jnp.float32), axis=-1).astype(x.dtype)
    c = PEER_TOKEN_BLOCK
    nb = T // c

    def block(args):
        xb, ib, gb = args
        ib = ib.reshape(c, H * K)
        gb = gb.reshape(c, H * K)
        u = jnp.take(u_tab, ib, axis=0)
        a = jax.nn.gelu(jnp.einsum('cd,ced->ce', xb, u), approximate=False) * gb
        vsel = jnp.take(v_tab, ib, axis=0)
        return jnp.einsum('ce,ced->cd', a, vsel)

    y = lax.map(block, (xt.reshape(nb, c, D), idx.reshape(nb, c, H, K), g.reshape(nb, c, H, K)))
    return y.reshape(B, S, D)


def setup_inputs(seed: int = 0) -> dict:
    key = jax.random.key(seed)
    ks = jax.random.split(key, 13)
    f32 = jnp.float32
    col_scale = jnp.concatenate([
        jnp.ones((2 * RET_WIDTH,), f32),
        jnp.full((RET_WIDTH,), BETA, f32),
        jnp.ones((RET_WIDTH,), f32),
        jnp.ones((SWA_WIDTH + SWA_KV_WIDTH,), f32),
        jnp.full((SWA_KV_WIDTH,), BETA, f32),
    ])
    x = jax.random.normal(ks[0], (BATCH, SEQ, D_MODEL), f32)
    w_in = jax.random.normal(ks[1], (DEPTH, D_MODEL, IN_COLS), f32) * (D_MODEL ** -0.5) * col_scale
    ret_gn_w = 1.0 + 0.02 * jax.random.normal(ks[2], (DEPTH, RET_WIDTH), f32)
    swa_sinks = 0.5 * jax.random.normal(ks[3], (DEPTH, SWA_Q_HEADS), f32)
    w_out = jax.random.normal(ks[4], (DEPTH, MIX_WIDTH, D_MODEL), f32) * (MIX_WIDTH ** -0.5) * BETA
    ln1_w = 1.0 + 0.02 * jax.random.normal(ks[5], (DEPTH, D_MODEL), f32)
    ln1_b = 0.02 * jax.random.normal(ks[6], (DEPTH, D_MODEL), f32)
    w_pq = jax.random.normal(ks[7], (DEPTH, D_MODEL, PEER_HEADS * PEER_QUERY_DIM), f32) * (D_MODEL ** -0.5)
    peer_sub_keys = jax.random.normal(ks[8], (DEPTH, PEER_HEADS, 2, PEER_N_KEYS, PEER_HALF), f32) * (PEER_HALF ** -0.5)
    peer_u = jax.random.normal(ks[9], (DEPTH, PEER_N_EXPERTS, D_MODEL), f32) * (D_MODEL ** -0.5) * BETA
    peer_v = jax.random.normal(ks[10], (DEPTH, PEER_N_EXPERTS, D_MODEL), f32) * BETA
    ln2_w = 1.0 + 0.02 * jax.random.normal(ks[11], (DEPTH, D_MODEL), f32)
    ln2_b = 0.02 * jax.random.normal(ks[12], (DEPTH, D_MODEL), f32)
    return {"x": x, "w_in": w_in, "ret_gn_w": ret_gn_w, "swa_sinks": swa_sinks, "w_out": w_out,
            "ln1_w": ln1_w, "ln1_b": ln1_b, "w_pq": w_pq, "peer_sub_keys": peer_sub_keys,
            "peer_u": peer_u, "peer_v": peer_v, "ln2_w": ln2_w, "ln2_b": ln2_b}


def reference(x, w_in, ret_gn_w, swa_sinks, w_out, ln1_w, ln1_b, w_pq, peer_sub_keys,
              peer_u, peer_v, ln2_w, ln2_b):
    B, S, D = x.shape
    pos = jnp.arange(S)
    o1 = RET_WIDTH
    splits = [o1, 2 * o1, 3 * o1, 4 * o1, 4 * o1 + SWA_WIDTH, 4 * o1 + SWA_WIDTH + SWA_KV_WIDTH]
    h = x
    for l in range(DEPTH):
        proj = h @ w_in[l]
        rq, rk, rv, rg, sq, sk, sv = jnp.split(proj, splits, axis=-1)
        rq = rotary(rq.reshape(B, S, RET_HEADS, RET_HEAD_DIM), pos)
        rk = rotary(rk.reshape(B, S, RET_HEADS, RET_HEAD_DIM), pos)
        rv = rv.reshape(B, S, RET_HEADS, RET_HEAD_DIM)
        y_ret = head_group_norm(retention(rq, rk, rv)).reshape(B, S, RET_WIDTH) * ret_gn_w[l]
        y_ret = jax.nn.silu(rg) * y_ret
        sq = rotary(sq.reshape(B, S, SWA_Q_HEADS, SWA_HEAD_DIM), pos)
        sk = rotary(sk.reshape(B, S, SWA_KV_HEADS, SWA_HEAD_DIM), pos)
        sv = sv.reshape(B, S, SWA_KV_HEADS, SWA_HEAD_DIM)
        y_swa = sliding_window_attention(sq, sk, sv, swa_sinks[l]).reshape(B, S, SWA_WIDTH)
        mix = jnp.concatenate([y_ret, y_swa], axis=-1) @ w_out[l]
        h = layer_norm(ALPHA * h + mix, ln1_w[l], ln1_b[l])
        ffn = peer_ffn(h, w_pq[l], peer_sub_keys[l], peer_u[l], peer_v[l])
        h = layer_norm(ALPHA * h + ffn, ln2_w[l], ln2_b[l])
    return h
```

```python
import functools
import math

import jax
import jax.numpy as jnp
from jax import lax
from jax.experimental import pallas as pl
from jax.experimental.pallas import tpu as pltpu

F32 = jnp.float32
BF16 = jnp.bfloat16

D_MODEL = 1024
RET_HEADS = 4
RET_HEAD_DIM = 128
RET_WIDTH = RET_HEADS * RET_HEAD_DIM
RET_CHUNK = 128
SWA_Q_HEADS = 8
SWA_KV_HEADS = 2
SWA_HEAD_DIM = 64
SWA_WIDTH = SWA_Q_HEADS * SWA_HEAD_DIM
SWA_KV_WIDTH = SWA_KV_HEADS * SWA_HEAD_DIM
WINDOW = 128
ROPE_THETA = 10000.0
PEER_HEADS = 8
PEER_N_KEYS = 128
PEER_N_EXPERTS = PEER_N_KEYS * PEER_N_KEYS
PEER_HALF = 128
PEER_TOPK = 16
LN_EPS = 1e-5
GN_EPS = 1e-6
DEPTH = 1
ALPHA = (2.0 * DEPTH) ** 0.25
NEG_INF = -1e30

LANES = 128
VMEM_LIMIT = 56 * 1024 * 1024

TM_IN = 512
TC_RET = 256
TQ_SWA = 512
TM_OUT = 512
TE_THR = 512
TM_PEER = 1024
TN_PEER = 1024
TB_PEER = 256


def _cparams(sem):
    return pltpu.CompilerParams(dimension_semantics=sem, vmem_limit_bytes=VMEM_LIMIT)


def _nt_dot(a, b):
    return lax.dot_general(a, b, (((1,), (1,)), ((), ())), preferred_element_type=F32)


def _inproj_kernel(x_ref, w_ref, cr_ref, sr_ref, cs_ref, ss_ref,
                   rq_ref, rk_ref, rv_ref, rg_ref, sq_ref, kv_ref):
    xb = x_ref[...].astype(BF16)
    cr, sr = cr_ref[...], sr_ref[...]
    cs, ss = cs_ref[...], ss_ref[...]

    def proj(c0, width):
        return jnp.dot(xb, w_ref[:, c0:c0 + width], preferred_element_type=F32)

    def rot_ret(p):
        return p * cr + pltpu.roll(p, 64, 1) * sr

    lane = lax.broadcasted_iota(jnp.int32, (x_ref.shape[0], LANES), 1)
    first_half = (lane % 64) < 32
    lo = lane < 64

    def rot_swa(p):
        swapped = jnp.where(first_half, pltpu.roll(p, 96, 1), pltpu.roll(p, 32, 1))
        return p * cs + swapped * ss

    pq = proj(0, RET_WIDTH)
    pk = proj(RET_WIDTH, RET_WIDTH)
    for h in range(RET_HEADS):
        c = slice(h * LANES, (h + 1) * LANES)
        rq_ref[:, c] = rot_ret(pq[:, c]).astype(BF16)
        rk_ref[:, c] = rot_ret(pk[:, c]).astype(BF16)
    rv_ref[...] = proj(2 * RET_WIDTH, RET_WIDTH).astype(BF16)
    rg_ref[...] = proj(3 * RET_WIDTH, RET_WIDTH)

    base = 4 * RET_WIDTH
    psq = proj(base, SWA_WIDTH)
    scale = SWA_HEAD_DIM ** -0.5
    for p in range(SWA_WIDTH // LANES):
        c = slice(p * LANES, (p + 1) * LANES)
        sq_ref[:, c] = (rot_swa(psq[:, c]) * scale).astype(BF16)

    pkv = proj(base + SWA_WIDTH, 2 * SWA_KV_WIDTH)
    sk = rot_swa(pkv[:, :LANES])
    sv = pkv[:, LANES:]
    zero = jnp.zeros_like(sk)
    for t, a in enumerate((sk, sv)):
        a_sw = pltpu.roll(a, 64, 1)
        o = t * 4 * LANES
        kv_ref[:, o + 0 * LANES:o + 1 * LANES] = jnp.where(lo, a, zero).astype(BF16)
        kv_ref[:, o + 1 * LANES:o + 2 * LANES] = jnp.where(lo, zero, a_sw).astype(BF16)
        kv_ref[:, o + 2 * LANES:o + 3 * LANES] = jnp.where(lo, a_sw, zero).astype(BF16)
        kv_ref[:, o + 3 * LANES:o + 4 * LANES] = jnp.where(lo, zero, a).astype(BF16)


def _inproj(xt, w_in, tabs, seq):
    T = xt.shape[0]
    tm = TM_IN
    nseq = seq // tm
    tab_spec = pl.BlockSpec((tm, LANES), lambda i: (i % nseq, 0))
    row = lambda w: pl.BlockSpec((tm, w), lambda i: (i, 0))
    return pl.pallas_call(
        _inproj_kernel,
        grid=(T // tm,),
        in_specs=[row(D_MODEL), pl.BlockSpec(w_in.shape, lambda i: (0, 0)),
                  tab_spec, tab_spec, tab_spec, tab_spec],
        out_specs=[row(RET_WIDTH), row(RET_WIDTH), row(RET_WIDTH), row(RET_WIDTH),
                   row(SWA_WIDTH), row(8 * LANES)],
        out_shape=[jax.ShapeDtypeStruct((T, RET_WIDTH), BF16),
                   jax.ShapeDtypeStruct((T, RET_WIDTH), BF16),
                   jax.ShapeDtypeStruct((T, RET_WIDTH), BF16),
                   jax.ShapeDtypeStruct((T, RET_WIDTH), F32),
                   jax.ShapeDtypeStruct((T, SWA_WIDTH), BF16),
                   jax.ShapeDtypeStruct((T, 8 * LANES), BF16)],
        compiler_params=_cparams(("parallel",)),
        name="inproj_rotary",
    )(xt, w_in, *tabs)


def _retention_kernel(rq_ref, rk_ref, rv_ref, rg_ref, gnw_ref, dec_ref, xi_ref, zeta_ref, g_ref,
                      y_ref, state_ref):
    @pl.when(pl.program_id(0) == 0)
    def _():
        state_ref[...] = jnp.zeros_like(state_ref)

    nb = rq_ref.shape[0]
    C = RET_CHUNK
    for b in range(nb):
        for h in range(RET_HEADS):
            cs = slice(h * LANES, (h + 1) * LANES)
            st = state_ref[b, h]
            for c in range(rq_ref.shape[1] // C):
                rs = slice(c * C, (c + 1) * C)
                q = rq_ref[b, rs, cs]
                k = rk_ref[b, rs, cs]
                v = rv_ref[b, rs, cs]
                s = _nt_dot(q, k) * dec_ref[h]
                y = jnp.dot(s.astype(BF16), v, preferred_element_type=F32)
                y = y + xi_ref[h] * jnp.dot(q, st.astype(BF16), preferred_element_type=F32)
                kz = (k.astype(F32) * zeta_ref[h]).T.astype(BF16)
                st = st * g_ref[h] + jnp.dot(kz, v, preferred_element_type=F32)
                mu = jnp.mean(y, axis=-1, keepdims=True)
                d = y - mu
                var = jnp.mean(d * d, axis=-1, keepdims=True)
                yn = d * lax.rsqrt(var + GN_EPS)
                gate = rg_ref[b, rs, cs]
                out = (gate * jax.nn.sigmoid(gate)) * (yn * gnw_ref[:, cs])
                y_ref[b, rs, cs] = out.astype(BF16)
            state_ref[b, h] = st


def _retention(rq, rk, rv, rg, gnw, consts, batch, seq):
    tc = TC_RET
    r3 = lambda a: a.reshape(batch, seq, RET_WIDTH)
    blk = pl.BlockSpec((batch, tc, RET_WIDTH), lambda j: (0, j, 0))
    cspec = pl.BlockSpec((RET_HEADS, RET_CHUNK, RET_CHUNK), lambda j: (0, 0, 0))
    y = pl.pallas_call(
        _retention_kernel,
        grid=(seq // tc,),
        in_specs=[blk, blk, blk, blk, pl.BlockSpec((1, RET_WIDTH), lambda j: (0, 0)),
                  cspec, cspec, cspec, cspec],
        out_specs=blk,
        out_shape=jax.ShapeDtypeStruct((batch, seq, RET_WIDTH), BF16),
        scratch_shapes=[pltpu.VMEM((batch, RET_HEADS, RET_HEAD_DIM, RET_HEAD_DIM), F32)],
        compiler_params=_cparams(("arbitrary",)),
        name="retention",
    )(r3(rq), r3(rk), r3(rv), r3(rg), gnw, *consts)
    return y.reshape(batch * seq, RET_WIDTH)


def _retention_consts():
    H, C, d = RET_HEADS, RET_CHUNK, RET_HEAD_DIM
    log_g = jnp.log(1.0 - 2.0 ** (-5.0 - jnp.arange(H, dtype=F32)))
    idx = jnp.arange(C, dtype=F32)
    diff = idx[:, None] - idx[None, :]
    scale = d ** -0.5
    decay = jnp.where(diff[None] >= 0, jnp.exp(jnp.maximum(diff, 0.0)[None] * log_g[:, None, None]), 0.0)
    zeta = jnp.exp((C - 1.0 - idx)[None] * log_g[:, None])
    xi = jnp.exp((idx + 1.0)[None] * log_g[:, None])
    g_chunk = jnp.exp(C * log_g)
    bc = lambda a: jnp.broadcast_to(a[:, :, None], (H, C, C)).astype(F32)
    return (decay * scale).astype(F32), bc(xi), bc(zeta * scale), \
        jnp.broadcast_to(g_chunk[:, None, None], (H, C, C)).astype(F32)


def _swa_kernel(sinks_ref, q_ref, kv_ref, kvp_ref, o_ref, *, blocks_per_seq):
    W = WINDOW
    i = pl.program_id(0)
    first = (i % blocks_per_seq) == 0
    qi = lax.broadcasted_iota(jnp.int32, (W, W), 0)
    kj = lax.broadcasted_iota(jnp.int32, (W, W), 1)
    cur_ok = kj <= qi
    prev_ok = kj > qi
    lane_lo = lax.broadcasted_iota(jnp.int32, (W, LANES), 1) < 64
    nq = q_ref.shape[0] // W
    for j in range(nq):
        rs = slice(j * W, (j + 1) * W)
        for p in range(SWA_WIDTH // LANES):
            kvh = p // 2
            q = q_ref[rs, p * LANES:(p + 1) * LANES]
            o_pair = None
            inv = []
            for e in range(2):
                kc = slice((kvh * 2 + e) * LANES, (kvh * 2 + e + 1) * LANES)
                vc = slice((4 + kvh * 2 + e) * LANES, (4 + kvh * 2 + e + 1) * LANES)
                if j == 0:
                    k_prev, v_prev = kvp_ref[:, kc], kvp_ref[:, vc]
                    p_ok = jnp.logical_and(prev_ok, jnp.logical_not(first))
                else:
                    ps = slice((j - 1) * W, j * W)
                    k_prev, v_prev = kv_ref[ps, kc], kv_ref[ps, vc]
                    p_ok = prev_ok
                sink = sinks_ref[2 * p + e]
                s_prev = jnp.where(p_ok, _nt_dot(q, k_prev), NEG_INF)
                s_cur = jnp.where(cur_ok, _nt_dot(q, kv_ref[rs, kc]), NEG_INF)
                m = jnp.maximum(jnp.maximum(jnp.max(s_prev, axis=-1, keepdims=True),
                                            jnp.max(s_cur, axis=-1, keepdims=True)), sink)
                p_prev = jnp.exp(s_prev - m)
                p_cur = jnp.exp(s_cur - m)
                denom = (jnp.sum(p_prev, axis=-1, keepdims=True) + jnp.sum(p_cur, axis=-1, keepdims=True)
                         + jnp.exp(sink - m))
                o = (jnp.dot(p_prev.astype(BF16), v_prev, preferred_element_type=F32)
                     + jnp.dot(p_cur.astype(BF16), kv_ref[rs, vc], preferred_element_type=F32))
                o_pair = o if o_pair is None else o_pair + o
                inv.append(1.0 / denom)
            o_ref[rs, p * LANES:(p + 1) * LANES] = (o_pair * jnp.where(lane_lo, inv[0], inv[1])).astype(BF16)


def _swa(sq, kv8, sinks, seq):
    T = sq.shape[0]
    tq = TQ_SWA
    nb = tq // WINDOW
    return pl.pallas_call(
        functools.partial(_swa_kernel, blocks_per_seq=seq // tq),
        grid=(T // tq,),
        in_specs=[pl.BlockSpec(memory_space=pltpu.SMEM),
                  pl.BlockSpec((tq, SWA_WIDTH), lambda i: (i, 0)),
                  pl.BlockSpec((tq, 8 * LANES), lambda i: (i, 0)),
                  pl.BlockSpec((WINDOW, 8 * LANES), lambda i: (jnp.maximum(i * nb - 1, 0), 0))],
        out_specs=pl.BlockSpec((tq, SWA_WIDTH), lambda i: (i, 0)),
        out_shape=jax.ShapeDtypeStruct((T, SWA_WIDTH), BF16),
        compiler_params=_cparams(("parallel",)),
        name="swa",
    )(sinks, sq, kv8, kv8)


def _layer_norm(y, w, b):
    mu = jnp.mean(y, axis=-1, keepdims=True)
    d = y - mu
    var = jnp.mean(d * d, axis=-1, keepdims=True)
    return d * lax.rsqrt(var + LN_EPS) * w + b


def _outproj_kernel(yr_ref, ys_ref, x_ref, wo_ref, lnw_ref, lnb_ref, wpq_ref, keys_ref,
                    h_ref, ht_ref, st_ref):
    mix = (jnp.dot(yr_ref[...], wo_ref[:RET_WIDTH, :], preferred_element_type=F32)
           + jnp.dot(ys_ref[...], wo_ref[RET_WIDTH:, :], preferred_element_type=F32))
    h = _layer_norm(ALPHA * x_ref[...] + mix, lnw_ref[...], lnb_ref[...])
    h_ref[...] = h
    ht_ref[...] = h.T.astype(BF16)
    q = jnp.dot(h.astype(BF16), wpq_ref[...], preferred_element_type=F32).astype(BF16)
    for hp in range(2 * PEER_HEADS):
        st_ref[hp] = _nt_dot(keys_ref[hp], q[:, hp * PEER_HALF:(hp + 1) * PEER_HALF])


def _outproj(y_ret, y_swa, xt, w_out, ln_w, ln_b, w_pq, keys):
    T = xt.shape[0]
    tm = TM_OUT
    row = lambda w: pl.BlockSpec((tm, w), lambda i: (i, 0))
    full = lambda a: pl.BlockSpec(a.shape, lambda i: (0,) * a.ndim)
    return pl.pallas_call(
        _outproj_kernel,
        grid=(T // tm,),
        in_specs=[row(RET_WIDTH), row(SWA_WIDTH), row(D_MODEL), full(w_out), full(ln_w), full(ln_b),
                  full(w_pq), full(keys)],
        out_specs=[row(D_MODEL), pl.BlockSpec((D_MODEL, tm), lambda i: (0, i)),
                   pl.BlockSpec((2 * PEER_HEADS, PEER_N_KEYS, tm), lambda i: (0, 0, i))],
        out_shape=[jax.ShapeDtypeStruct((T, D_MODEL), F32),
                   jax.ShapeDtypeStruct((D_MODEL, T), BF16),
                   jax.ShapeDtypeStruct((2 * PEER_HEADS, PEER_N_KEYS, T), F32)],
        compiler_params=_cparams(("parallel",)),
        name="outproj_ln1_scores",
    )(y_ret, y_swa, xt, w_out, ln_w, ln_b, w_pq, keys)


def _top_rows(s, k):
    rows = []
    w = s
    for a in range(k):
        m = jnp.max(w, axis=0, keepdims=True)
        rows.append(m)
        if a + 1 < k:
            w = jnp.where(w == m, -jnp.inf, w)
    return rows


def _stack_rows(rows, lo, hi):
    sub = lax.broadcasted_iota(jnp.int32, (8, rows[0].shape[1]), 0)
    out = jnp.full((8, rows[0].shape[1]), -jnp.inf, F32)
    for r in range(lo, hi):
        out = jnp.where(sub == (r - lo), rows[r], out)
    return out


def _threshold_kernel(s_ref, stat_ref):
    K = PEER_TOPK
    L = LANES

    @pl.loop(0, s_ref.shape[2] // L)
    def _(blk):
        ls = pl.ds(pl.multiple_of(blk * L, L), L)
        for h in range(PEER_HEADS):
            t1 = _top_rows(s_ref[2 * h, :, ls], K)
            t2 = _top_rows(s_ref[2 * h + 1, :, ls], K)
            t1_lo, t1_hi = _stack_rows(t1, 0, 8), _stack_rows(t1, 8, 16)
            t2_hi = _stack_rows(t2, 8, 16)
            sub = lax.broadcasted_iota(jnp.int32, (8, L), 0)
            cands = [t1_lo + t2[0], t1_hi + t2[0], t2_hi + t1[0]]
            for b in range(1, 8):
                a_max = K // (b + 1) - 1
                cands.append(jnp.where(sub <= a_max, t1_lo + t2[b], -jnp.inf))
            w = list(cands)
            tau = None
            for a in range(K):
                m = w[0]
                for c in w[1:]:
                    m = jnp.maximum(m, c)
                m = jnp.max(m, axis=0, keepdims=True)
                tau = m
                if a + 1 < K:
                    w = [jnp.where(c == m, -jnp.inf, c) for c in w]
            top = t1[0] + t2[0]
            z = None
            for c in cands:
                e = jnp.where(c >= tau, jnp.exp(c - top), 0.0)
                z = e if z is None else z + e
            z = jnp.sum(z, axis=0, keepdims=True)
            stat_ref[h, 0:1, ls] = tau
            stat_ref[h, 1:2, ls] = t1[0]
            stat_ref[h, 2:3, ls] = t2[0]
            stat_ref[h, 3:4, ls] = 1.0 / z
            stat_ref[h, 4:8, ls] = jnp.zeros((4, L), F32)


def _thresholds(scores_t):
    T = scores_t.shape[2]
    te = TE_THR
    return pl.pallas_call(
        _threshold_kernel,
        grid=(T // te,),
        in_specs=[pl.BlockSpec((2 * PEER_HEADS, PEER_N_KEYS, te), lambda i: (0, 0, i))],
        out_specs=pl.BlockSpec((PEER_HEADS, 8, te), lambda i: (0, 0, i)),
        out_shape=jax.ShapeDtypeStruct((PEER_HEADS, 8, T), F32),
        compiler_params=_cparams(("parallel",)),
        name="peer_threshold",
    )(scores_t)


def _gelu_exact(z):
    return 0.5 * z * (1.0 + lax.erf(z * (2.0 ** -0.5)))


def _peer_kernel(u_ref, ht_ref, vt_ref, s1_ref, s2_ref, stat_ref, h_ref, lnw_ref, lnb_ref,
                 out_ref, e2_ref, acc_ref):
    j = pl.program_id(1)
    nj = pl.num_programs(1)
    tn = u_ref.shape[0]
    tm = ht_ref.shape[1]
    groups = tn // PEER_N_KEYS

    @pl.when(j == 0)
    def _():
        acc_ref[...] = jnp.zeros_like(acc_ref)
        for h in range(PEER_HEADS):
            e2_ref[h] = jnp.exp(s2_ref[h] - stat_ref[h, 2:3, :])

    for blk in range(tm // TB_PEER):
        cb = slice(blk * TB_PEER, (blk + 1) * TB_PEER)
        z = jnp.dot(u_ref[...], ht_ref[:, cb], preferred_element_type=F32)
        parts = []
        for k in range(groups):
            zk = z[k * PEER_N_KEYS:(k + 1) * PEER_N_KEYS, :]
            halves = []
            for half in range(TB_PEER // LANES):
                ls = slice(blk * TB_PEER + half * LANES, blk * TB_PEER + (half + 1) * LANES)
                gate = jnp.zeros((PEER_N_KEYS, LANES), F32)
                for h in range(PEER_HEADS):
                    s1 = s1_ref[h, k:k + 1, ls]
                    e1 = jnp.exp(s1 - stat_ref[h, 1:2, ls]) * stat_ref[h, 3:4, ls]
                    tau = stat_ref[h, 0:1, ls]
                    sel = (s1 + s2_ref[h, :, ls]) >= tau
                    gate = gate + jnp.where(sel, e2_ref[h, :, ls], 0.0) * e1
                zh = zk[:, half * LANES:(half + 1) * LANES]
                halves.append((_gelu_exact(zh) * gate).astype(BF16))
            parts.append(jnp.concatenate(halves, axis=1))
        a = jnp.concatenate(parts, axis=0)
        acc_ref[:, cb] += jnp.dot(vt_ref[...], a, preferred_element_type=F32)

    @pl.when(j == nj - 1)
    def _():
        y = ALPHA * h_ref[...] + acc_ref[...].T
        out_ref[...] = _layer_norm(y, lnw_ref[...], lnb_ref[...])


def _peer(u, ht, vt, scores_t, stats, h, ln_w, ln_b):
    T = h.shape[0]
    tm, tn = TM_PEER, TN_PEER
    groups = tn // PEER_N_KEYS
    s4 = scores_t.reshape(PEER_HEADS, 2, PEER_N_KEYS, T)
    return pl.pallas_call(
        _peer_kernel,
        grid=(T // tm, PEER_N_EXPERTS // tn),
        in_specs=[pl.BlockSpec((tn, D_MODEL), lambda i, j: (j, 0)),
                  pl.BlockSpec((D_MODEL, tm), lambda i, j: (0, i)),
                  pl.BlockSpec((D_MODEL, tn), lambda i, j: (0, j)),
                  pl.BlockSpec((PEER_HEADS, None, groups, tm), lambda i, j: (0, 0, j, i)),
                  pl.BlockSpec((PEER_HEADS, None, PEER_N_KEYS, tm), lambda i, j: (0, 1, 0, i)),
                  pl.BlockSpec((PEER_HEADS, 8, tm), lambda i, j: (0, 0, i)),
                  pl.BlockSpec((tm, D_MODEL), lambda i, j: (i, 0)),
                  pl.BlockSpec((1, D_MODEL), lambda i, j: (0, 0)),
                  pl.BlockSpec((1, D_MODEL), lambda i, j: (0, 0))],
        out_specs=pl.BlockSpec((tm, D_MODEL), lambda i, j: (i, 0)),
        out_shape=jax.ShapeDtypeStruct((T, D_MODEL), F32),
        scratch_shapes=[pltpu.VMEM((PEER_HEADS, PEER_N_KEYS, tm), F32),
                        pltpu.VMEM((D_MODEL, tm), F32)],
        compiler_params=_cparams(("parallel", "arbitrary")),
        name="peer_ffn_ln2",
    )(u, ht, vt, s4, s4, stats, h, ln_w, ln_b)


def _rotary_tables(seq):
    pos = jnp.arange(seq, dtype=F32)

    def tab(d):
        inv = 1.0 / (ROPE_THETA ** (jnp.arange(0, d, 2, dtype=F32) / d))
        ang = pos[:, None] * inv[None, :]
        cos, sin = jnp.cos(ang), jnp.sin(ang)
        reps = LANES // d
        c = jnp.tile(jnp.concatenate([cos, cos], axis=-1), (1, reps))
        s = jnp.tile(jnp.concatenate([-sin, sin], axis=-1), (1, reps))
        return c, s

    cr, sr = tab(RET_HEAD_DIM)
    cs, ss = tab(SWA_HEAD_DIM)
    return cr, sr, cs, ss


def kernel(x, w_in, ret_gn_w, swa_sinks, w_out, ln1_w, ln1_b, w_pq, peer_sub_keys, peer_u, peer_v, ln2_w, ln2_b):
    B, S, D = x.shape
    assert D == D_MODEL and w_in.shape[0] == DEPTH
    T = B * S
    h = x.reshape(T, D)
    tabs = _rotary_tables(S)
    ret_consts = _retention_consts()
    for l in range(DEPTH):
        rq, rk, rv, rg, sq, kv8 = _inproj(h, w_in[l].astype(BF16), tabs, S)
        y_ret = _retention(rq, rk, rv, rg, ret_gn_w[l].reshape(1, RET_WIDTH), ret_consts, B, S)
        y_swa = _swa(sq, kv8, swa_sinks[l], S)
        keys = peer_sub_keys[l].reshape(2 * PEER_HEADS, PEER_N_KEYS, PEER_HALF).astype(BF16)
        h1, h1_t, scores_t = _outproj(y_ret, y_swa, h, w_out[l].astype(BF16), ln1_w[l].reshape(1, D),
                                      ln1_b[l].reshape(1, D), w_pq[l].astype(BF16), keys)
        stats = _thresholds(scores_t)
        h = _peer(peer_u[l].astype(BF16), h1_t, peer_v[l].T.astype(BF16), scores_t, stats, h1,
                  ln2_w[l].reshape(1, D), ln2_b[l].reshape(1, D))
    return h.reshape(B, S, D)
```

```python
import functools
import math

import jax
import jax.numpy as jnp
from jax import lax
from jax.experimental import pallas as pl
from jax.experimental.pallas import tpu as pltpu

F32 = jnp.float32
BF16 = jnp.bfloat16

D_MODEL = 1024
RET_HEADS = 4
RET_HEAD_DIM = 128
RET_WIDTH = RET_HEADS * RET_HEAD_DIM
RET_CHUNK = 128
SWA_Q_HEADS = 8
SWA_KV_HEADS = 2
SWA_HEAD_DIM = 64
SWA_WIDTH = SWA_Q_HEADS * SWA_HEAD_DIM
SWA_KV_WIDTH = SWA_KV_HEADS * SWA_HEAD_DIM
WINDOW = 128
ROPE_THETA = 10000.0
PEER_HEADS = 8
PEER_N_KEYS = 128
PEER_N_EXPERTS = PEER_N_KEYS * PEER_N_KEYS
PEER_HALF = 128
PEER_TOPK = 16
LN_EPS = 1e-5
GN_EPS = 1e-6
DEPTH = 1
ALPHA = (2.0 * DEPTH) ** 0.25
NEG_INF = -1e30

LANES = 128
VMEM_LIMIT = 56 * 1024 * 1024

TM_IN = 512
TC_RET = 256
TQ_SWA = 512
TM_OUT = 512
TE_THR = 512
TM_PEER = 1024
TN_PEER = 1024
TB_PEER = 256


def _cparams(sem):
    return pltpu.CompilerParams(dimension_semantics=sem, vmem_limit_bytes=VMEM_LIMIT)


def _nt_dot(a, b):
    return lax.dot_general(a, b, (((1,), (1,)), ((), ())), preferred_element_type=F32)


def _inproj_kernel(x_ref, w_ref, cr_ref, sr_ref, cs_ref, ss_ref,
                   rq_ref, rk_ref, rv_ref, rg_ref, sq_ref, kv_ref):
    xb = x_ref[...].astype(BF16)
    cr, sr = cr_ref[...], sr_ref[...]
    cs, ss = cs_ref[...], ss_ref[...]

    def proj(c0, width):
        return jnp.dot(xb, w_ref[:, c0:c0 + width], preferred_element_type=F32)

    def rot_ret(p):
        return p * cr + pltpu.roll(p, 64, 1) * sr

    lane = lax.broadcasted_iota(jnp.int32, (x_ref.shape[0], LANES), 1)
    first_half = (lane % 64) < 32
    lo = lane < 64

    def rot_swa(p):
        swapped = jnp.where(first_half, pltpu.roll(p, 96, 1), pltpu.roll(p, 32, 1))
        return p * cs + swapped * ss

    pq = proj(0, RET_WIDTH)
    pk = proj(RET_WIDTH, RET_WIDTH)
    for h in range(RET_HEADS):
        c = slice(h * LANES, (h + 1) * LANES)
        rq_ref[:, c] = rot_ret(pq[:, c]).astype(BF16)
        rk_ref[:, c] = rot_ret(pk[:, c]).astype(BF16)
    rv_ref[...] = proj(2 * RET_WIDTH, RET_WIDTH).astype(BF16)
    rg_ref[...] = proj(3 * RET_WIDTH, RET_WIDTH)

    base = 4 * RET_WIDTH
    psq = proj(base, SWA_WIDTH)
    scale = SWA_HEAD_DIM ** -0.5
    for p in range(SWA_WIDTH // LANES):
        c = slice(p * LANES, (p + 1) * LANES)
        sq_ref[:, c] = (rot_swa(psq[:, c]) * scale).astype(BF16)

    pkv = proj(base + SWA_WIDTH, 2 * SWA_KV_WIDTH)
    sk = rot_swa(pkv[:, :LANES])
    sv = pkv[:, LANES:]
    zero = jnp.zeros_like(sk)
    for t, a in enumerate((sk, sv)):
        a_sw = pltpu.roll(a, 64, 1)
        o = t * 4 * LANES
        kv_ref[:, o + 0 * LANES:o + 1 * LANES] = jnp.where(lo, a, zero).astype(BF16)
        kv_ref[:, o + 1 * LANES:o + 2 * LANES] = jnp.where(lo, zero, a_sw).astype(BF16)
        kv_ref[:, o + 2 * LANES:o + 3 * LANES] = jnp.where(lo, a_sw, zero).astype(BF16)
        kv_ref[:, o + 3 * LANES:o + 4 * LANES] = jnp.where(lo, zero, a).astype(BF16)


def _inproj(xt, w_in, tabs, seq):
    T = xt.shape[0]
    tm = TM_IN
    nseq = seq // tm
    tab_spec = pl.BlockSpec((tm, LANES), lambda i: (i % nseq, 0))
    row = lambda w: pl.BlockSpec((tm, w), lambda i: (i, 0))
    return pl.pallas_call(
        _inproj_kernel,
        grid=(T // tm,),
        in_specs=[row(D_MODEL), pl.BlockSpec(w_in.shape, lambda i: (0, 0)),
                  tab_spec, tab_spec, tab_spec, tab_spec],
        out_specs=[row(RET_WIDTH), row(RET_WIDTH), row(RET_WIDTH), row(RET_WIDTH),
                   row(SWA_WIDTH), row(8 * LANES)],
        out_shape=[jax.ShapeDtypeStruct((T, RET_WIDTH), BF16),
                   jax.ShapeDtypeStruct((T, RET_WIDTH), BF16),
                   jax.ShapeDtypeStruct((T, RET_WIDTH), BF16),
                   jax.ShapeDtypeStruct((T, RET_WIDTH), F32),
                   jax.ShapeDtypeStruct((T, SWA_WIDTH), BF16),
                   jax.ShapeDtypeStruct((T, 8 * LANES), BF16)],
        compiler_params=_cparams(("parallel",)),
        name="inproj_rotary",
    )(xt, w_in, *tabs)


def _retention_kernel(rq_ref, rk_ref, rv_ref, rg_ref, gnw_ref, dec_ref, xi_ref, zeta_ref, g_ref,
                      y_ref, state_ref):
    @pl.when(pl.program_id(0) == 0)
    def _():
        state_ref[...] = jnp.zeros_like(state_ref)

    nb = rq_ref.shape[0]
    C = RET_CHUNK
    for b in range(nb):
        for h in range(RET_HEADS):
            cs = slice(h * LANES, (h + 1) * LANES)
            st = state_ref[b, h]
            for c in range(rq_ref.shape[1] // C):
                rs = slice(c * C, (c + 1) * C)
                q = rq_ref[b, rs, cs]
                k = rk_ref[b, rs, cs]
                v = rv_ref[b, rs, cs]
                s = _nt_dot(q, k) * dec_ref[h]
                y = jnp.dot(s.astype(BF16), v, preferred_element_type=F32)
                y = y + xi_ref[h] * jnp.dot(q, st.astype(BF16), preferred_element_type=F32)
                kz = (k.astype(F32) * zeta_ref[h]).T.astype(BF16)
                st = st * g_ref[h] + jnp.dot(kz, v, preferred_element_type=F32)
                mu = jnp.mean(y, axis=-1, keepdims=True)
                d = y - mu
                var = jnp.mean(d * d, axis=-1, keepdims=True)
                yn = d * lax.rsqrt(var + GN_EPS)
                gate = rg_ref[b, rs, cs]
                out = (gate * jax.nn.sigmoid(gate)) * (yn * gnw_ref[:, cs])
                y_ref[b, rs, cs] = out.astype(BF16)
            state_ref[b, h] = st


def _retention(rq, rk, rv, rg, gnw, consts, batch, seq):
    tc = TC_RET
    r3 = lambda a: a.reshape(batch, seq, RET_WIDTH)
    blk = pl.BlockSpec((batch, tc, RET_WIDTH), lambda j: (0, j, 0))
    cspec = pl.BlockSpec((RET_HEADS, RET_CHUNK, RET_CHUNK), lambda j: (0, 0, 0))
    y = pl.pallas_call(
        _retention_kernel,
        grid=(seq // tc,),
        in_specs=[blk, blk, blk, blk, pl.BlockSpec((1, RET_WIDTH), lambda j: (0, 0)),
                  cspec, cspec, cspec, cspec],
        out_specs=blk,
        out_shape=jax.ShapeDtypeStruct((batch, seq, RET_WIDTH), BF16),
        scratch_shapes=[pltpu.VMEM((batch, RET_HEADS, RET_HEAD_DIM, RET_HEAD_DIM), F32)],
        compiler_params=_cparams(("arbitrary",)),
        name="retention",
    )(r3(rq), r3(rk), r3(rv), r3(rg), gnw, *consts)
    return y.reshape(batch * seq, RET_WIDTH)


def _retention_consts():
    H, C, d = RET_HEADS, RET_CHUNK, RET_HEAD_DIM
    log_g = jnp.log(1.0 - 2.0 ** (-5.0 - jnp.arange(H, dtype=F32)))
    idx = jnp.arange(C, dtype=F32)
    diff = idx[:, None] - idx[None, :]
    scale = d ** -0.5
    decay = jnp.where(diff[None] >= 0, jnp.exp(jnp.maximum(diff, 0.0)[None] * log_g[:, None, None]), 0.0)
    zeta = jnp.exp((C - 1.0 - idx)[None] * log_g[:, None])
    xi = jnp.exp((idx + 1.0)[None] * log_g[:, None])
    g_chunk = jnp.exp(C * log_g)
    bc = lambda a: jnp.broadcast_to(a[:, :, None], (H, C, C)).astype(F32)
    return (decay * scale).astype(F32), bc(xi), bc(zeta * scale), \
        jnp.broadcast_to(g_chunk[:, None, None], (H, C, C)).astype(F32)


def _swa_kernel(sinks_ref, q_ref, kv_ref, kvp_ref, o_ref, *, blocks_per_seq):
    W = WINDOW
    i = pl.program_id(0)
    first = (i % blocks_per_seq) == 0
    qi = lax.broadcasted_iota(jnp.int32, (W, W), 0)
    kj = lax.broadcasted_iota(jnp.int32, (W, W), 1)
    cur_ok = kj <= qi
    prev_ok = kj > qi
    lane_lo = lax.broadcasted_iota(jnp.int32, (W, LANES), 1) < 64
    nq = q_ref.shape[0] // W
    for j in range(nq):
        rs = slice(j * W, (j + 1) * W)
        for p in range(SWA_WIDTH // LANES):
            kvh = p // 2
            q = q_ref[rs, p * LANES:(p + 1) * LANES]
            o_pair = None
            inv = []
            for e in range(2):
                kc = slice((kvh * 2 + e) * LANES, (kvh * 2 + e + 1) * LANES)
                vc = slice((4 + kvh * 2 + e) * LANES, (4 + kvh * 2 + e + 1) * LANES)
                if j == 0:
                    k_prev, v_prev = kvp_ref[:, kc], kvp_ref[:, vc]
                    p_ok = jnp.logical_and(prev_ok, jnp.logical_not(first))
                else:
                    ps = slice((j - 1) * W, j * W)
                    k_prev, v_prev = kv_ref[ps, kc], kv_ref[ps, vc]
                    p_ok = prev_ok
                sink = sinks_ref[2 * p + e]
                s_prev = jnp.where(p_ok, _nt_dot(q, k_prev), NEG_INF)
                s_cur = jnp.where(cur_ok, _nt_dot(q, kv_ref[rs, kc]), NEG_INF)
                m = jnp.maximum(jnp.maximum(jnp.max(s_prev, axis=-1, keepdims=True),
                                            jnp.max(s_cur, axis=-1, keepdims=True)), sink)
                p_prev = jnp.exp(s_prev - m)
                p_cur = jnp.exp(s_cur - m)
                denom = (jnp.sum(p_prev, axis=-1, keepdims=True) + jnp.sum(p_cur, axis=-1, keepdims=True)
                         + jnp.exp(sink - m))
                o = (jnp.dot(p_prev.astype(BF16), v_prev, preferred_element_type=F32)
                     + jnp.dot(p_cur.astype(BF16), kv_ref[rs, vc], preferred_element_type=F32))
                o_pair = o if o_pair is None else o_pair + o
                inv.append(1.0 / denom)
            o_ref[rs, p * LANES:(p + 1) * LANES] = (o_pair * jnp.where(lane_lo, inv[0], inv[1])).astype(BF16)


def _swa(sq, kv8, sinks, seq):
    T = sq.shape[0]
    tq = TQ_SWA
    nb = tq // WINDOW
    return pl.pallas_call(
        functools.partial(_swa_kernel, blocks_per_seq=seq // tq),
        grid=(T // tq,),
        in_specs=[pl.BlockSpec(memory_space=pltpu.SMEM),
                  pl.BlockSpec((tq, SWA_WIDTH), lambda i: (i, 0)),
                  pl.BlockSpec((tq, 8 * LANES), lambda i: (i, 0)),
                  pl.BlockSpec((WINDOW, 8 * LANES), lambda i: (jnp.maximum(i * nb - 1, 0), 0))],
        out_specs=pl.BlockSpec((tq, SWA_WIDTH), lambda i: (i, 0)),
        out_shape=jax.ShapeDtypeStruct((T, SWA_WIDTH), BF16),
        compiler_params=_cparams(("parallel",)),
        name="swa",
    )(sinks, sq, kv8, kv8)


def _layer_norm(y, w, b):
    mu = jnp.mean(y, axis=-1, keepdims=True)
    d = y - mu
    var = jnp.mean(d * d, axis=-1, keepdims=True)
    return d * lax.rsqrt(var + LN_EPS) * w + b


def _outproj_kernel(yr_ref, ys_ref, x_ref, wo_ref, lnw_ref, lnb_ref, wpq_ref, keys_ref,
                    h_ref, ht_ref, st_ref):
    mix = (jnp.dot(yr_ref[...], wo_ref[:RET_WIDTH, :], preferred_element_type=F32)
           + jnp.dot(ys_ref[...], wo_ref[RET_WIDTH:, :], preferred_element_type=F32))
    h = _layer_norm(ALPHA * x_ref[...] + mix, lnw_ref[...], lnb_ref[...])
    h_ref[...] = h
    ht_ref[...] = h.T.astype(BF16)
    q = jnp.dot(h.astype(BF16), wpq_ref[...], preferred_element_type=F32).astype(BF16)
    for hp in range(2 * PEER_HEADS):
        st_ref[hp] = _nt_dot(keys_ref[hp], q[:, hp * PEER_HALF:(hp + 1) * PEER_HALF])


def _outproj(y_ret, y_swa, xt, w_out, ln_w, ln_b, w_pq, keys):
    T = xt.shape[0]
    tm = TM_OUT
    row = lambda w: pl.BlockSpec((tm, w), lambda i: (i, 0))
    full = lambda a: pl.BlockSpec(a.shape, lambda i: (0,) * a.ndim)
    return pl.pallas_call(
        _outproj_kernel,
        grid=(T // tm,),
        in_specs=[row(RET_WIDTH), row(SWA_WIDTH), row(D_MODEL), full(w_out), full(ln_w), full(ln_b),
                  full(w_pq), full(keys)],
        out_specs=[row(D_MODEL), pl.BlockSpec((D_MODEL, tm), lambda i: (0, i)),
                   pl.BlockSpec((2 * PEER_HEADS, PEER_N_KEYS, tm), lambda i: (0, 0, i))],
        out_shape=[jax.ShapeDtypeStruct((T, D_MODEL), F32),
                   jax.ShapeDtypeStruct((D_MODEL, T), BF16),
                   jax.ShapeDtypeStruct((2 * PEER_HEADS, PEER_N_KEYS, T), F32)],
        compiler_params=_cparams(("parallel",)),
        name="outproj_ln1_scores",
    )(y_ret, y_swa, xt, w_out, ln_w, ln_b, w_pq, keys)


def _top_rows(s, k):
    rows = []
    w = s
    rank = jnp.full(s.shape, float(k), F32)
    for a in range(k):
        m = jnp.max(w, axis=0, keepdims=True)
        rows.append(m)
        hit = w == m
        rank = jnp.where(hit, float(a), rank)
        if a + 1 < k:
            w = jnp.where(hit, -jnp.inf, w)
    return rows, rank


def _stack_rows(rows, lo, hi):
    sub = lax.broadcasted_iota(jnp.int32, (8, rows[0].shape[1]), 0)
    out = jnp.full((8, rows[0].shape[1]), -jnp.inf, F32)
    for r in range(lo, hi):
        out = jnp.where(sub == (r - lo), rows[r], out)
    return out


def _threshold_kernel(s_ref, cnt_ref, e1_ref, r2_ref, e2_ref):
    K = PEER_TOPK
    L = LANES

    @pl.loop(0, s_ref.shape[2] // L)
    def _(blk):
        ls = pl.ds(pl.multiple_of(blk * L, L), L)
        for h in range(PEER_HEADS):
            s1 = s_ref[2 * h, :, ls]
            s2 = s_ref[2 * h + 1, :, ls]
            t1, r1 = _top_rows(s1, K)
            t2, r2 = _top_rows(s2, K)
            t1_lo, t1_hi = _stack_rows(t1, 0, 8), _stack_rows(t1, 8, 16)
            t2_lo, t2_hi = _stack_rows(t2, 0, 8), _stack_rows(t2, 8, 16)
            sub = lax.broadcasted_iota(jnp.int32, (8, L), 0)
            cands = [t1_lo + t2[0], t1_hi + t2[0], t2_hi + t1[0]]
            for b in range(1, 8):
                a_max = K // (b + 1) - 1
                cands.append(jnp.where(sub <= a_max, t1_lo + t2[b], -jnp.inf))
            w = list(cands)
            tau = None
            for a in range(K):
                m = w[0]
                for c in w[1:]:
                    m = jnp.maximum(m, c)
                m = jnp.max(m, axis=0, keepdims=True)
                tau = m
                if a + 1 < K:
                    w = [jnp.where(c == m, -jnp.inf, c) for c in w]
            top = t1[0] + t2[0]
            z = None
            for c in cands:
                e = jnp.where(c >= tau, jnp.exp(c - top), 0.0)
                z = e if z is None else z + e
            inv_z = 1.0 / jnp.sum(z, axis=0, keepdims=True)
            cnt = jnp.zeros(s1.shape, F32)
            for a in range(K):
                b_max = K // (a + 1) - 1
                n = jnp.where(jnp.logical_and(t1[a] + t2_lo >= tau, sub <= b_max), 1.0, 0.0)
                if a == 0:
                    n = n + jnp.where(t1[a] + t2_hi >= tau, 1.0, 0.0)
                cnt = jnp.where(r1 == float(a), jnp.sum(n, axis=0, keepdims=True), cnt)
            cnt_ref[h, :, ls] = cnt
            e1_ref[h, :, ls] = jnp.exp(s1 - t1[0]) * inv_z
            r2_ref[h, :, ls] = pltpu.bitcast(r2.astype(BF16), jnp.uint32)
            e2_ref[h, :, ls] = pltpu.bitcast(jnp.exp(s2 - t2[0]).astype(BF16), jnp.uint32)


def _thresholds(scores_t):
    T = scores_t.shape[2]
    te = TE_THR
    blk = pl.BlockSpec((PEER_HEADS, PEER_N_KEYS, te), lambda i: (0, 0, i))
    shp = jax.ShapeDtypeStruct((PEER_HEADS, PEER_N_KEYS, T), F32)
    pblk = pl.BlockSpec((PEER_HEADS, PEER_N_KEYS // 2, te), lambda i: (0, 0, i))
    pshp = jax.ShapeDtypeStruct((PEER_HEADS, PEER_N_KEYS // 2, T), jnp.uint32)
    return pl.pallas_call(
        _threshold_kernel,
        grid=(T // te,),
        in_specs=[pl.BlockSpec((2 * PEER_HEADS, PEER_N_KEYS, te), lambda i: (0, 0, i))],
        out_specs=[blk, blk, pblk, pblk],
        out_shape=[shp, shp, pshp, pshp],
        compiler_params=_cparams(("parallel",)),
        name="peer_threshold",
    )(scores_t)


def _gelu_exact(z):
    return 0.5 * z * (1.0 + lax.erf(z * (2.0 ** -0.5)))


def _peer_kernel(u_ref, ht_ref, vt_ref, cnt_ref, e1_ref, r2_ref, e2_ref, h_ref, lnw_ref, lnb_ref,
                 out_ref, z_ref, a_ref, acc_ref):
    j = pl.program_id(1)
    nj = pl.num_programs(1)
    tn = u_ref.shape[0]
    tm = ht_ref.shape[1]
    groups = tn // PEER_N_KEYS
    nblk = tm // TB_PEER
    SUB = 16
    zero = jnp.zeros((SUB, LANES), BF16)

    @pl.when(j == 0)
    def _():
        acc_ref[...] = jnp.zeros_like(acc_ref)

    def first_dot(blk):
        cb = slice(blk * TB_PEER, (blk + 1) * TB_PEER)
        z_ref[blk % 2] = jnp.dot(u_ref[...], ht_ref[:, cb], preferred_element_type=F32)

    first_dot(0)
    for blk in range(nblk):
        if blk + 1 < nblk:
            first_dot(blk + 1)
        slot = blk % 2
        for k in range(groups):
            for half in range(TB_PEER // LANES):
                ls = slice(blk * TB_PEER + half * LANES, blk * TB_PEER + (half + 1) * LANES)
                zs = slice(half * LANES, (half + 1) * LANES)
                gate = [zero] * (PEER_N_KEYS // SUB)
                for h in range(PEER_HEADS):
                    cnt = jnp.broadcast_to(cnt_ref[h, k:k + 1, ls], (SUB, LANES)).astype(BF16)
                    e1 = jnp.broadcast_to(e1_ref[h, k:k + 1, ls], (SUB, LANES)).astype(BF16)
                    for rt in range(PEER_N_KEYS // SUB):
                        pr = slice(rt * SUB // 2, (rt + 1) * SUB // 2)
                        r2 = pltpu.bitcast(r2_ref[h, pr, ls], BF16)
                        e2 = pltpu.bitcast(e2_ref[h, pr, ls], BF16)
                        gate[rt] = gate[rt] + jnp.where(r2 < cnt, e2, zero) * e1
                for rt in range(PEER_N_KEYS // SUB):
                    rr = slice(k * PEER_N_KEYS + rt * SUB, k * PEER_N_KEYS + (rt + 1) * SUB)
                    pr = slice((k * PEER_N_KEYS + rt * SUB) // 2, (k * PEER_N_KEYS + (rt + 1) * SUB) // 2)
                    a = _gelu_exact(z_ref[slot, rr, zs]).astype(BF16) * gate[rt]
                    a_ref[slot, pr, zs] = pltpu.bitcast(a, jnp.uint32)
        cb = slice(blk * TB_PEER, (blk + 1) * TB_PEER)
        acc_ref[:, cb] += jnp.dot(vt_ref[...], pltpu.bitcast(a_ref[slot], BF16),
                                  preferred_element_type=F32)

    @pl.when(j == nj - 1)
    def _():
        y = ALPHA * h_ref[...] + acc_ref[...].T
        out_ref[...] = _layer_norm(y, lnw_ref[...], lnb_ref[...])


def _peer(u, ht, vt, cnt, e1, r2, e2, h, ln_w, ln_b):
    T = h.shape[0]
    tm, tn = TM_PEER, TN_PEER
    groups = tn // PEER_N_KEYS
    rows = pl.BlockSpec((PEER_HEADS, groups, tm), lambda i, j: (0, j, i))
    keys = pl.BlockSpec((PEER_HEADS, PEER_N_KEYS // 2, tm), lambda i, j: (0, 0, i))
    return pl.pallas_call(
        _peer_kernel,
        grid=(T // tm, PEER_N_EXPERTS // tn),
        in_specs=[pl.BlockSpec((tn, D_MODEL), lambda i, j: (j, 0)),
                  pl.BlockSpec((D_MODEL, tm), lambda i, j: (0, i)),
                  pl.BlockSpec((D_MODEL, tn), lambda i, j: (0, j)),
                  rows, rows, keys, keys,
                  pl.BlockSpec((tm, D_MODEL), lambda i, j: (i, 0)),
                  pl.BlockSpec((1, D_MODEL), lambda i, j: (0, 0)),
                  pl.BlockSpec((1, D_MODEL), lambda i, j: (0, 0))],
        out_specs=pl.BlockSpec((tm, D_MODEL), lambda i, j: (i, 0)),
        out_shape=jax.ShapeDtypeStruct((T, D_MODEL), F32),
        scratch_shapes=[pltpu.VMEM((2, tn, TB_PEER), F32),
                        pltpu.VMEM((2, tn // 2, TB_PEER), jnp.uint32),
                        pltpu.VMEM((D_MODEL, tm), F32)],
        compiler_params=_cparams(("parallel", "arbitrary")),
        name="peer_ffn_ln2",
    )(u, ht, vt, cnt, e1, r2, e2, h, ln_w, ln_b)


def _rotary_tables(seq):
    pos = jnp.arange(seq, dtype=F32)

    def tab(d):
        inv = 1.0 / (ROPE_THETA ** (jnp.arange(0, d, 2, dtype=F32) / d))
        ang = pos[:, None] * inv[None, :]
        cos, sin = jnp.cos(ang), jnp.sin(ang)
        reps = LANES // d
        c = jnp.tile(jnp.concatenate([cos, cos], axis=-1), (1, reps))
        s = jnp.tile(jnp.concatenate([-sin, sin], axis=-1), (1, reps))
        return c, s

    cr, sr = tab(RET_HEAD_DIM)
    cs, ss = tab(SWA_HEAD_DIM)
    return cr, sr, cs, ss


def kernel(x, w_in, ret_gn_w, swa_sinks, w_out, ln1_w, ln1_b, w_pq, peer_sub_keys, peer_u, peer_v, ln2_w, ln2_b):
    B, S, D = x.shape
    assert D == D_MODEL and w_in.shape[0] == DEPTH
    T = B * S
    h = x.reshape(T, D)
    tabs = _rotary_tables(S)
    ret_consts = _retention_consts()
    for l in range(DEPTH):
        rq, rk, rv, rg, sq, kv8 = _inproj(h, w_in[l].astype(BF16), tabs, S)
        y_ret = _retention(rq, rk, rv, rg, ret_gn_w[l].reshape(1, RET_WIDTH), ret_consts, B, S)
        y_swa = _swa(sq, kv8, swa_sinks[l], S)
        keys = peer_sub_keys[l].reshape(2 * PEER_HEADS, PEER_N_KEYS, PEER_HALF).astype(BF16)
        h1, h1_t, scores_t = _outproj(y_ret, y_swa, h, w_out[l].astype(BF16), ln1_w[l].reshape(1, D),
                                      ln1_b[l].reshape(1, D), w_pq[l].astype(BF16), keys)
        cnt, e1, r2, e2 = _thresholds(scores_t)
        h = _peer(peer_u[l].astype(BF16), h1_t, peer_v[l].T.astype(BF16), cnt, e1, r2, e2, h1,
                  ln2_w[l].reshape(1, D), ln2_b[l].reshape(1, D))
    return h.reshape(B, S, D)
```

```python
import functools
import math

import jax
import jax.numpy as jnp
from jax import lax
from jax.experimental import pallas as pl
from jax.experimental.pallas import tpu as pltpu

F32 = jnp.float32
BF16 = jnp.bfloat16

D_MODEL = 1024
RET_HEADS = 4
RET_HEAD_DIM = 128
RET_WIDTH = RET_HEADS * RET_HEAD_DIM
RET_CHUNK = 128
SWA_Q_HEADS = 8
SWA_KV_HEADS = 2
SWA_HEAD_DIM = 64
SWA_WIDTH = SWA_Q_HEADS * SWA_HEAD_DIM
SWA_KV_WIDTH = SWA_KV_HEADS * SWA_HEAD_DIM
WINDOW = 128
ROPE_THETA = 10000.0
PEER_HEADS = 8
PEER_N_KEYS = 128
PEER_N_EXPERTS = PEER_N_KEYS * PEER_N_KEYS
PEER_HALF = 128
PEER_TOPK = 16
LN_EPS = 1e-5
GN_EPS = 1e-6
DEPTH = 1
ALPHA = (2.0 * DEPTH) ** 0.25
NEG_INF = -1e30

LANES = 128
VMEM_LIMIT = 58 * 1024 * 1024

TM_IN = 512
TC_RET = 256
TQ_SWA = 512
TM_OUT = 512
TE_THR = 512
TM_PEER = 2048
TN_PEER = 1024
TB_PEER = 256


def _cparams(sem, **kw):
    return pltpu.CompilerParams(dimension_semantics=sem, vmem_limit_bytes=VMEM_LIMIT, **kw)


def _nt_dot(a, b):
    return lax.dot_general(a, b, (((1,), (1,)), ((), ())), preferred_element_type=F32)


def _inproj_kernel(x_ref, w_ref, cr_ref, sr_ref, cs_ref, ss_ref,
                   rq_ref, rk_ref, rv_ref, rg_ref, sq_ref, kv_ref):
    xb = x_ref[...].astype(BF16)
    cr, sr = cr_ref[...], sr_ref[...]
    cs, ss = cs_ref[...], ss_ref[...]

    def proj(c0, width):
        return jnp.dot(xb, w_ref[:, c0:c0 + width], preferred_element_type=F32)

    def rot_ret(p):
        return p * cr + pltpu.roll(p, 64, 1) * sr

    lane = lax.broadcasted_iota(jnp.int32, (x_ref.shape[0], LANES), 1)
    first_half = (lane % 64) < 32
    lo = lane < 64

    def rot_swa(p):
        swapped = jnp.where(first_half, pltpu.roll(p, 96, 1), pltpu.roll(p, 32, 1))
        return p * cs + swapped * ss

    pq = proj(0, RET_WIDTH)
    pk = proj(RET_WIDTH, RET_WIDTH)
    for h in range(RET_HEADS):
        c = slice(h * LANES, (h + 1) * LANES)
        rq_ref[:, c] = rot_ret(pq[:, c]).astype(BF16)
        rk_ref[:, c] = rot_ret(pk[:, c]).astype(BF16)
    rv_ref[...] = proj(2 * RET_WIDTH, RET_WIDTH).astype(BF16)
    rg_ref[...] = proj(3 * RET_WIDTH, RET_WIDTH)

    base = 4 * RET_WIDTH
    psq = proj(base, SWA_WIDTH)
    scale = SWA_HEAD_DIM ** -0.5
    for p in range(SWA_WIDTH // LANES):
        c = slice(p * LANES, (p + 1) * LANES)
        sq_ref[:, c] = (rot_swa(psq[:, c]) * scale).astype(BF16)

    pkv = proj(base + SWA_WIDTH, 2 * SWA_KV_WIDTH)
    sk = rot_swa(pkv[:, :LANES])
    sv = pkv[:, LANES:]
    zero = jnp.zeros_like(sk)
    for t, a in enumerate((sk, sv)):
        a_sw = pltpu.roll(a, 64, 1)
        o = t * 4 * LANES
        kv_ref[:, o + 0 * LANES:o + 1 * LANES] = jnp.where(lo, a, zero).astype(BF16)
        kv_ref[:, o + 1 * LANES:o + 2 * LANES] = jnp.where(lo, zero, a_sw).astype(BF16)
        kv_ref[:, o + 2 * LANES:o + 3 * LANES] = jnp.where(lo, a_sw, zero).astype(BF16)
        kv_ref[:, o + 3 * LANES:o + 4 * LANES] = jnp.where(lo, zero, a).astype(BF16)


def _inproj(xt, w_in, tabs, seq):
    T = xt.shape[0]
    tm = TM_IN
    nseq = seq // tm
    tab_spec = pl.BlockSpec((tm, LANES), lambda i: (i % nseq, 0))
    row = lambda w: pl.BlockSpec((tm, w), lambda i: (i, 0))
    return pl.pallas_call(
        _inproj_kernel,
        grid=(T // tm,),
        in_specs=[row(D_MODEL), pl.BlockSpec(w_in.shape, lambda i: (0, 0)),
                  tab_spec, tab_spec, tab_spec, tab_spec],
        out_specs=[row(RET_WIDTH), row(RET_WIDTH), row(RET_WIDTH), row(RET_WIDTH),
                   row(SWA_WIDTH), row(8 * LANES)],
        out_shape=[jax.ShapeDtypeStruct((T, RET_WIDTH), BF16),
                   jax.ShapeDtypeStruct((T, RET_WIDTH), BF16),
                   jax.ShapeDtypeStruct((T, RET_WIDTH), BF16),
                   jax.ShapeDtypeStruct((T, RET_WIDTH), F32),
                   jax.ShapeDtypeStruct((T, SWA_WIDTH), BF16),
                   jax.ShapeDtypeStruct((T, 8 * LANES), BF16)],
        compiler_params=_cparams(("parallel",)),
        name="inproj_rotary",
    )(xt, w_in, *tabs)


def _retention_kernel(rq_ref, rk_ref, rv_ref, rg_ref, gnw_ref, dec_ref, xi_ref, zeta_ref, g_ref,
                      y_ref, state_ref):
    @pl.when(pl.program_id(0) == 0)
    def _():
        state_ref[...] = jnp.zeros_like(state_ref)

    nb = rq_ref.shape[0]
    C = RET_CHUNK
    for b in range(nb):
        for h in range(RET_HEADS):
            cs = slice(h * LANES, (h + 1) * LANES)
            st = state_ref[b, h]
            for c in range(rq_ref.shape[1] // C):
                rs = slice(c * C, (c + 1) * C)
                q = rq_ref[b, rs, cs]
                k = rk_ref[b, rs, cs]
                v = rv_ref[b, rs, cs]
                s = _nt_dot(q, k) * dec_ref[h]
                y = jnp.dot(s.astype(BF16), v, preferred_element_type=F32)
                y = y + xi_ref[h] * jnp.dot(q, st.astype(BF16), preferred_element_type=F32)
                kz = (k.astype(F32) * zeta_ref[h]).T.astype(BF16)
                st = st * g_ref[h] + jnp.dot(kz, v, preferred_element_type=F32)
                mu = jnp.mean(y, axis=-1, keepdims=True)
                d = y - mu
                var = jnp.mean(d * d, axis=-1, keepdims=True)
                yn = d * lax.rsqrt(var + GN_EPS)
                gate = rg_ref[b, rs, cs]
                out = (gate * jax.nn.sigmoid(gate)) * (yn * gnw_ref[:, cs])
                y_ref[b, rs, cs] = out.astype(BF16)
            state_ref[b, h] = st


def _retention(rq, rk, rv, rg, gnw, consts, batch, seq):
    tc = TC_RET
    r3 = lambda a: a.reshape(batch, seq, RET_WIDTH)
    blk = pl.BlockSpec((batch, tc, RET_WIDTH), lambda j: (0, j, 0))
    cspec = pl.BlockSpec((RET_HEADS, RET_CHUNK, RET_CHUNK), lambda j: (0, 0, 0))
    y = pl.pallas_call(
        _retention_kernel,
        grid=(seq // tc,),
        in_specs=[blk, blk, blk, blk, pl.BlockSpec((1, RET_WIDTH), lambda j: (0, 0)),
                  cspec, cspec, cspec, cspec],
        out_specs=blk,
        out_shape=jax.ShapeDtypeStruct((batch, seq, RET_WIDTH), BF16),
        scratch_shapes=[pltpu.VMEM((batch, RET_HEADS, RET_HEAD_DIM, RET_HEAD_DIM), F32)],
        compiler_params=_cparams(("arbitrary",)),
        name="retention",
    )(r3(rq), r3(rk), r3(rv), r3(rg), gnw, *consts)
    return y.reshape(batch * seq, RET_WIDTH)


def _retention_consts():
    H, C, d = RET_HEADS, RET_CHUNK, RET_HEAD_DIM
    log_g = jnp.log(1.0 - 2.0 ** (-5.0 - jnp.arange(H, dtype=F32)))
    idx = jnp.arange(C, dtype=F32)
    diff = idx[:, None] - idx[None, :]
    scale = d ** -0.5
    decay = jnp.where(diff[None] >= 0, jnp.exp(jnp.maximum(diff, 0.0)[None] * log_g[:, None, None]), 0.0)
    zeta = jnp.exp((C - 1.0 - idx)[None] * log_g[:, None])
    xi = jnp.exp((idx + 1.0)[None] * log_g[:, None])
    g_chunk = jnp.exp(C * log_g)
    bc = lambda a: jnp.broadcast_to(a[:, :, None], (H, C, C)).astype(F32)
    return (decay * scale).astype(F32), bc(xi), bc(zeta * scale), \
        jnp.broadcast_to(g_chunk[:, None, None], (H, C, C)).astype(F32)


def _swa_kernel(sinks_ref, q_ref, kv_ref, kvp_ref, o_ref, *, blocks_per_seq):
    W = WINDOW
    i = pl.program_id(0)
    first = (i % blocks_per_seq) == 0
    qi = lax.broadcasted_iota(jnp.int32, (W, W), 0)
    kj = lax.broadcasted_iota(jnp.int32, (W, W), 1)
    cur_ok = kj <= qi
    prev_ok = kj > qi
    lane_lo = lax.broadcasted_iota(jnp.int32, (W, LANES), 1) < 64
    nq = q_ref.shape[0] // W
    for j in range(nq):
        rs = slice(j * W, (j + 1) * W)
        for p in range(SWA_WIDTH // LANES):
            kvh = p // 2
            q = q_ref[rs, p * LANES:(p + 1) * LANES]
            o_pair = None
            inv = []
            for e in range(2):
                kc = slice((kvh * 2 + e) * LANES, (kvh * 2 + e + 1) * LANES)
                vc = slice((4 + kvh * 2 + e) * LANES, (4 + kvh * 2 + e + 1) * LANES)
                if j == 0:
                    k_prev, v_prev = kvp_ref[:, kc], kvp_ref[:, vc]
                    p_ok = jnp.logical_and(prev_ok, jnp.logical_not(first))
                else:
                    ps = slice((j - 1) * W, j * W)
                    k_prev, v_prev = kv_ref[ps, kc], kv_ref[ps, vc]
                    p_ok = prev_ok
                sink = sinks_ref[2 * p + e]
                s_prev = jnp.where(p_ok, _nt_dot(q, k_prev), NEG_INF)
                s_cur = jnp.where(cur_ok, _nt_dot(q, kv_ref[rs, kc]), NEG_INF)
                m = jnp.maximum(jnp.maximum(jnp.max(s_prev, axis=-1, keepdims=True),
                                            jnp.max(s_cur, axis=-1, keepdims=True)), sink)
                p_prev = jnp.exp(s_prev - m)
                p_cur = jnp.exp(s_cur - m)
                denom = (jnp.sum(p_prev, axis=-1, keepdims=True) + jnp.sum(p_cur, axis=-1, keepdims=True)
                         + jnp.exp(sink - m))
                o = (jnp.dot(p_prev.astype(BF16), v_prev, preferred_element_type=F32)
                     + jnp.dot(p_cur.astype(BF16), kv_ref[rs, vc], preferred_element_type=F32))
                o_pair = o if o_pair is None else o_pair + o
                inv.append(1.0 / denom)
            o_ref[rs, p * LANES:(p + 1) * LANES] = (o_pair * jnp.where(lane_lo, inv[0], inv[1])).astype(BF16)


def _swa(sq, kv8, sinks, seq):
    T = sq.shape[0]
    tq = TQ_SWA
    nb = tq // WINDOW
    return pl.pallas_call(
        functools.partial(_swa_kernel, blocks_per_seq=seq // tq),
        grid=(T // tq,),
        in_specs=[pl.BlockSpec(memory_space=pltpu.SMEM),
                  pl.BlockSpec((tq, SWA_WIDTH), lambda i: (i, 0)),
                  pl.BlockSpec((tq, 8 * LANES), lambda i: (i, 0)),
                  pl.BlockSpec((WINDOW, 8 * LANES), lambda i: (jnp.maximum(i * nb - 1, 0), 0))],
        out_specs=pl.BlockSpec((tq, SWA_WIDTH), lambda i: (i, 0)),
        out_shape=jax.ShapeDtypeStruct((T, SWA_WIDTH), BF16),
        compiler_params=_cparams(("parallel",)),
        name="swa",
    )(sinks, sq, kv8, kv8)


def _layer_norm(y, w, b):
    mu = jnp.mean(y, axis=-1, keepdims=True)
    d = y - mu
    var = jnp.mean(d * d, axis=-1, keepdims=True)
    return d * lax.rsqrt(var + LN_EPS) * w + b


def _outproj_kernel(yr_ref, ys_ref, x_ref, wo_ref, lnw_ref, lnb_ref, wpq_ref, keys_ref,
                    h_ref, ht_ref, st_ref):
    mix = (jnp.dot(yr_ref[...], wo_ref[:RET_WIDTH, :], preferred_element_type=F32)
           + jnp.dot(ys_ref[...], wo_ref[RET_WIDTH:, :], preferred_element_type=F32))
    h = _layer_norm(ALPHA * x_ref[...] + mix, lnw_ref[...], lnb_ref[...])
    h_ref[...] = h
    ht_ref[...] = pltpu.bitcast(h.T.astype(BF16), jnp.uint32)
    q = jnp.dot(h.astype(BF16), wpq_ref[...], preferred_element_type=F32).astype(BF16)
    for hp in range(2 * PEER_HEADS):
        st_ref[hp] = _nt_dot(keys_ref[hp], q[:, hp * PEER_HALF:(hp + 1) * PEER_HALF])


def _outproj(y_ret, y_swa, xt, w_out, ln_w, ln_b, w_pq, keys):
    T = xt.shape[0]
    tm = TM_OUT
    row = lambda w: pl.BlockSpec((tm, w), lambda i: (i, 0))
    full = lambda a: pl.BlockSpec(a.shape, lambda i: (0,) * a.ndim)
    return pl.pallas_call(
        _outproj_kernel,
        grid=(T // tm,),
        in_specs=[row(RET_WIDTH), row(SWA_WIDTH), row(D_MODEL), full(w_out), full(ln_w), full(ln_b),
                  full(w_pq), full(keys)],
        out_specs=[row(D_MODEL), pl.BlockSpec((D_MODEL // 2, tm), lambda i: (0, i)),
                   pl.BlockSpec((2 * PEER_HEADS, PEER_N_KEYS, tm), lambda i: (0, 0, i))],
        out_shape=[jax.ShapeDtypeStruct((T, D_MODEL), F32),
                   jax.ShapeDtypeStruct((D_MODEL // 2, T), jnp.uint32),
                   jax.ShapeDtypeStruct((2 * PEER_HEADS, PEER_N_KEYS, T), F32)],
        compiler_params=_cparams(("parallel",)),
        name="outproj_ln1_scores",
    )(y_ret, y_swa, xt, w_out, ln_w, ln_b, w_pq, keys)


def _top_rows(s, k):
    rows = []
    w = s
    rank = jnp.full(s.shape, float(k), F32)
    for a in range(k):
        m = jnp.max(w, axis=0, keepdims=True)
        rows.append(m)
        hit = w == m
        rank = jnp.where(hit, float(a), rank)
        if a + 1 < k:
            w = jnp.where(hit, -jnp.inf, w)
    return rows, rank


def _stack_rows(rows, lo, hi):
    sub = lax.broadcasted_iota(jnp.int32, (8, rows[0].shape[1]), 0)
    out = jnp.full((8, rows[0].shape[1]), -jnp.inf, F32)
    for r in range(lo, hi):
        out = jnp.where(sub == (r - lo), rows[r], out)
    return out


def _threshold_kernel(s_ref, cnt_ref, e1_ref, r2_ref, e2_ref):
    K = PEER_TOPK
    L = LANES

    @pl.loop(0, s_ref.shape[2] // L)
    def _(blk):
        ls = pl.ds(pl.multiple_of(blk * L, L), L)
        for h in range(PEER_HEADS):
            s1 = s_ref[2 * h, :, ls]
            s2 = s_ref[2 * h + 1, :, ls]
            t1, r1 = _top_rows(s1, K)
            t2, r2 = _top_rows(s2, K)
            t1_lo, t1_hi = _stack_rows(t1, 0, 8), _stack_rows(t1, 8, 16)
            t2_lo, t2_hi = _stack_rows(t2, 0, 8), _stack_rows(t2, 8, 16)
            sub = lax.broadcasted_iota(jnp.int32, (8, L), 0)
            cands = [t1_lo + t2[0], t1_hi + t2[0], t2_hi + t1[0]]
            for b in range(1, 8):
                a_max = K // (b + 1) - 1
                cands.append(jnp.where(sub <= a_max, t1_lo + t2[b], -jnp.inf))
            w = list(cands)
            tau = None
            for a in range(K):
                m = w[0]
                for c in w[1:]:
                    m = jnp.maximum(m, c)
                m = jnp.max(m, axis=0, keepdims=True)
                tau = m
                if a + 1 < K:
                    w = [jnp.where(c == m, -jnp.inf, c) for c in w]
            top = t1[0] + t2[0]
            z = None
            for c in cands:
                e = jnp.where(c >= tau, jnp.exp(c - top), 0.0)
                z = e if z is None else z + e
            inv_z = 1.0 / jnp.sum(z, axis=0, keepdims=True)
            cnt = jnp.zeros(s1.shape, F32)
            for a in range(K):
                b_max = K // (a + 1) - 1
                n = jnp.where(jnp.logical_and(t1[a] + t2_lo >= tau, sub <= b_max), 1.0, 0.0)
                if a == 0:
                    n = n + jnp.where(t1[a] + t2_hi >= tau, 1.0, 0.0)
                cnt = jnp.where(r1 == float(a), jnp.sum(n, axis=0, keepdims=True), cnt)
            cnt_ref[h, :, ls] = cnt
            e1_ref[h, :, ls] = jnp.exp(s1 - t1[0]) * inv_z
            r2_ref[h, :, ls] = pltpu.bitcast(r2.astype(BF16), jnp.uint32)
            e2_ref[h, :, ls] = pltpu.bitcast(jnp.exp(s2 - t2[0]).astype(BF16), jnp.uint32)


def _thresholds(scores_t):
    T = scores_t.shape[2]
    te = TE_THR
    blk = pl.BlockSpec((PEER_HEADS, PEER_N_KEYS, te), lambda i: (0, 0, i))
    shp = jax.ShapeDtypeStruct((PEER_HEADS, PEER_N_KEYS, T), F32)
    pblk = pl.BlockSpec((PEER_HEADS, PEER_N_KEYS // 2, te), lambda i: (0, 0, i))
    pshp = jax.ShapeDtypeStruct((PEER_HEADS, PEER_N_KEYS // 2, T), jnp.uint32)
    return pl.pallas_call(
        _threshold_kernel,
        grid=(T // te,),
        in_specs=[pl.BlockSpec((2 * PEER_HEADS, PEER_N_KEYS, te), lambda i: (0, 0, i))],
        out_specs=[blk, blk, pblk, pblk],
        out_shape=[shp, shp, pshp, pshp],
        compiler_params=_cparams(("parallel",)),
        name="peer_threshold",
    )(scores_t)


def _gelu_exact(z):
    return 0.5 * z * (1.0 + lax.erf(z * (2.0 ** -0.5)))


def _peer_kernel(u_ref, ht_ref, vt_ref, cnt_ref, e1_ref, r2_ref, e2_ref, h_ref, lnw_ref, lnb_ref,
                 out_ref, z_ref, a_ref, acc_ref):
    j = pl.program_id(1)
    nj = pl.num_programs(1)
    tn = 2 * u_ref.shape[0]
    tm = ht_ref.shape[1]
    groups = tn // PEER_N_KEYS
    nblk = tm // TB_PEER
    SUB = 16
    tiles = PEER_N_KEYS // SUB
    zero = jnp.zeros((SUB, LANES), BF16)

    @pl.when(j == 0)
    def _():
        acc_ref[...] = jnp.zeros_like(acc_ref)

    def cols(blk):
        return slice(blk * TB_PEER, (blk + 1) * TB_PEER)

    def scores(blk):
        z_ref[blk % 2] = jnp.dot(pltpu.bitcast(u_ref[...], BF16), pltpu.bitcast(ht_ref[:, cols(blk)], BF16),
                                 preferred_element_type=F32)

    def activations(blk):
        slot = blk % 2
        for k in range(groups):
            for half in range(TB_PEER // LANES):
                ls = slice(blk * TB_PEER + half * LANES, blk * TB_PEER + (half + 1) * LANES)
                zs = slice(half * LANES, (half + 1) * LANES)
                gate = [zero] * tiles
                for h in range(PEER_HEADS):
                    cnt = jnp.broadcast_to(cnt_ref[h, k:k + 1, ls], (SUB, LANES)).astype(BF16)
                    e1 = jnp.broadcast_to(e1_ref[h, k:k + 1, ls], (SUB, LANES)).astype(BF16)
                    for rt in range(tiles):
                        pr = slice(rt * SUB // 2, (rt + 1) * SUB // 2)
                        r2 = pltpu.bitcast(r2_ref[h, pr, ls], BF16)
                        e2 = pltpu.bitcast(e2_ref[h, pr, ls], BF16)
                        gate[rt] = gate[rt] + jnp.where(r2 < cnt, e2, zero) * e1
                for rt in range(tiles):
                    rr = slice(k * PEER_N_KEYS + rt * SUB, k * PEER_N_KEYS + (rt + 1) * SUB)
                    pr = slice((k * PEER_N_KEYS + rt * SUB) // 2, (k * PEER_N_KEYS + (rt + 1) * SUB) // 2)
                    a = _gelu_exact(z_ref[slot, rr, zs]).astype(BF16) * gate[rt]
                    a_ref[slot, pr, zs] = pltpu.bitcast(a, jnp.uint32)

    def project(blk):
        acc_ref[:, cols(blk)] += jnp.dot(pltpu.bitcast(vt_ref[...], BF16),
                                         pltpu.bitcast(a_ref[blk % 2], BF16),
                                         preferred_element_type=F32)

    scores(0)
    for blk in range(nblk):
        if blk + 1 < nblk:
            scores(blk + 1)
        activations(blk)
        project(blk)

    @pl.when(j == nj - 1)
    def _():
        for blk in range(nblk):
            rows = cols(blk)
            y = ALPHA * h_ref[rows, :] + acc_ref[:, rows].T
            out_ref[rows, :] = _layer_norm(y, lnw_ref[...], lnb_ref[...])


def _peer(u, ht, vt, cnt, e1, r2, e2, h, ln_w, ln_b):
    T = h.shape[0]
    tm, tn = TM_PEER, TN_PEER
    groups = tn // PEER_N_KEYS
    once = dict(pipeline_mode=pl.Buffered(1))
    rows = pl.BlockSpec((PEER_HEADS, groups, tm), lambda i, j: (0, j, i))
    keys = pl.BlockSpec((PEER_HEADS, PEER_N_KEYS // 2, tm), lambda i, j: (0, 0, i), **once)
    return pl.pallas_call(
        _peer_kernel,
        grid=(T // tm, PEER_N_EXPERTS // tn),
        in_specs=[pl.BlockSpec((tn // 2, D_MODEL), lambda i, j: (j, 0)),
                  pl.BlockSpec((D_MODEL // 2, tm), lambda i, j: (0, i), **once),
                  pl.BlockSpec((D_MODEL // 2, tn), lambda i, j: (0, j)),
                  rows, rows, keys, keys,
                  pl.BlockSpec((tm, D_MODEL), lambda i, j: (i, 0), **once),
                  pl.BlockSpec((1, D_MODEL), lambda i, j: (0, 0)),
                  pl.BlockSpec((1, D_MODEL), lambda i, j: (0, 0))],
        out_specs=pl.BlockSpec((tm, D_MODEL), lambda i, j: (i, 0), **once),
        out_shape=jax.ShapeDtypeStruct((T, D_MODEL), F32),
        scratch_shapes=[pltpu.VMEM((2, tn, TB_PEER), F32),
                        pltpu.VMEM((2, tn // 2, TB_PEER), jnp.uint32),
                        pltpu.VMEM((D_MODEL, tm), F32)],
        compiler_params=_cparams(("parallel", "arbitrary")),
        name="peer_ffn_ln2",
    )(u, ht, vt, cnt, e1, r2, e2, h, ln_w, ln_b)


def _rotary_tables(seq):
    pos = jnp.arange(seq, dtype=F32)

    def tab(d):
        inv = 1.0 / (ROPE_THETA ** (jnp.arange(0, d, 2, dtype=F32) / d))
        ang = pos[:, None] * inv[None, :]
        cos, sin = jnp.cos(ang), jnp.sin(ang)
        reps = LANES // d
        c = jnp.tile(jnp.concatenate([cos, cos], axis=-1), (1, reps))
        s = jnp.tile(jnp.concatenate([-sin, sin], axis=-1), (1, reps))
        return c, s

    cr, sr = tab(RET_HEAD_DIM)
    cs, ss = tab(SWA_HEAD_DIM)
    return cr, sr, cs, ss


def _pack_row_pairs(a):
    m2, n = a.shape
    return lax.bitcast_convert_type(a.reshape(m2 // 2, 2, n).transpose(0, 2, 1), jnp.uint32)


def kernel(x, w_in, ret_gn_w, swa_sinks, w_out, ln1_w, ln1_b, w_pq, peer_sub_keys, peer_u, peer_v, ln2_w, ln2_b):
    B, S, D = x.shape
    assert D == D_MODEL and w_in.shape[0] == DEPTH
    T = B * S
    h = x.reshape(T, D)
    tabs = _rotary_tables(S)
    ret_consts = _retention_consts()
    for l in range(DEPTH):
        rq, rk, rv, rg, sq, kv8 = _inproj(h, w_in[l].astype(BF16), tabs, S)
        y_ret = _retention(rq, rk, rv, rg, ret_gn_w[l].reshape(1, RET_WIDTH), ret_consts, B, S)
        y_swa = _swa(sq, kv8, swa_sinks[l], S)
        keys = peer_sub_keys[l].reshape(2 * PEER_HEADS, PEER_N_KEYS, PEER_HALF).astype(BF16)
        h1, h1_t, scores_t = _outproj(y_ret, y_swa, h, w_out[l].astype(BF16), ln1_w[l].reshape(1, D),
                                      ln1_b[l].reshape(1, D), w_pq[l].astype(BF16), keys)
        cnt, e1, r2, e2 = _thresholds(scores_t)
        h = _peer(_pack_row_pairs(peer_u[l].astype(BF16)), h1_t, _pack_row_pairs(peer_v[l].T.astype(BF16)),
                  cnt, e1, r2, e2, h1, ln2_w[l].reshape(1, D), ln2_b[l].reshape(1, D))
    return h.reshape(B, S, D)
```

```python
import functools
import math

import jax
import jax.numpy as jnp
from jax import lax
from jax.experimental import pallas as pl
from jax.experimental.pallas import tpu as pltpu

F32 = jnp.float32
BF16 = jnp.bfloat16

D_MODEL = 1024
RET_HEADS = 4
RET_HEAD_DIM = 128
RET_WIDTH = RET_HEADS * RET_HEAD_DIM
RET_CHUNK = 128
SWA_Q_HEADS = 8
SWA_KV_HEADS = 2
SWA_HEAD_DIM = 64
SWA_WIDTH = SWA_Q_HEADS * SWA_HEAD_DIM
SWA_KV_WIDTH = SWA_KV_HEADS * SWA_HEAD_DIM
WINDOW = 128
ROPE_THETA = 10000.0
PEER_HEADS = 8
PEER_N_KEYS = 128
PEER_N_EXPERTS = PEER_N_KEYS * PEER_N_KEYS
PEER_HALF = 128
PEER_TOPK = 16
LN_EPS = 1e-5
GN_EPS = 1e-6
DEPTH = 1
ALPHA = (2.0 * DEPTH) ** 0.25
NEG_INF = -1e30

LANES = 128
VMEM_LIMIT = 58 * 1024 * 1024

TM_IN = 512
TC_RET = 256
TQ_SWA = 512
TM_OUT = 512
TE_THR = 512
TM_PEER = 2048
TN_PEER = 1024
TB_PEER = 256


def _cparams(sem, **kw):
    return pltpu.CompilerParams(dimension_semantics=sem, vmem_limit_bytes=VMEM_LIMIT, **kw)


def _nt_dot(a, b):
    return lax.dot_general(a, b, (((1,), (1,)), ((), ())), preferred_element_type=F32)


def _inproj_kernel(x_ref, w_ref, cr_ref, sr_ref, cs_ref, ss_ref,
                   rq_ref, rk_ref, rv_ref, rg_ref, sq_ref, kv_ref):
    xb = x_ref[...].astype(BF16)
    cr, sr = cr_ref[...], sr_ref[...]
    cs, ss = cs_ref[...], ss_ref[...]

    def proj(c0, width):
        return jnp.dot(xb, w_ref[:, c0:c0 + width], preferred_element_type=F32)

    def rot_ret(p):
        return p * cr + pltpu.roll(p, 64, 1) * sr

    lane = lax.broadcasted_iota(jnp.int32, (x_ref.shape[0], LANES), 1)
    first_half = (lane % 64) < 32
    lo = lane < 64

    def rot_swa(p):
        swapped = jnp.where(first_half, pltpu.roll(p, 96, 1), pltpu.roll(p, 32, 1))
        return p * cs + swapped * ss

    pq = proj(0, RET_WIDTH)
    pk = proj(RET_WIDTH, RET_WIDTH)
    for h in range(RET_HEADS):
        c = slice(h * LANES, (h + 1) * LANES)
        rq_ref[:, c] = rot_ret(pq[:, c]).astype(BF16)
        rk_ref[:, c] = rot_ret(pk[:, c]).astype(BF16)
    rv_ref[...] = proj(2 * RET_WIDTH, RET_WIDTH).astype(BF16)
    rg_ref[...] = proj(3 * RET_WIDTH, RET_WIDTH)

    base = 4 * RET_WIDTH
    psq = proj(base, SWA_WIDTH)
    scale = SWA_HEAD_DIM ** -0.5
    for p in range(SWA_WIDTH // LANES):
        c = slice(p * LANES, (p + 1) * LANES)
        sq_ref[:, c] = (rot_swa(psq[:, c]) * scale).astype(BF16)

    pkv = proj(base + SWA_WIDTH, 2 * SWA_KV_WIDTH)
    sk = rot_swa(pkv[:, :LANES])
    sv = pkv[:, LANES:]
    zero = jnp.zeros_like(sk)
    for t, a in enumerate((sk, sv)):
        a_sw = pltpu.roll(a, 64, 1)
        o = t * 4 * LANES
        kv_ref[:, o + 0 * LANES:o + 1 * LANES] = jnp.where(lo, a, zero).astype(BF16)
        kv_ref[:, o + 1 * LANES:o + 2 * LANES] = jnp.where(lo, zero, a_sw).astype(BF16)
        kv_ref[:, o + 2 * LANES:o + 3 * LANES] = jnp.where(lo, a_sw, zero).astype(BF16)
        kv_ref[:, o + 3 * LANES:o + 4 * LANES] = jnp.where(lo, zero, a).astype(BF16)


def _inproj(xt, w_in, tabs, seq):
    T = xt.shape[0]
    tm = TM_IN
    nseq = seq // tm
    tab_spec = pl.BlockSpec((tm, LANES), lambda i: (i % nseq, 0))
    row = lambda w: pl.BlockSpec((tm, w), lambda i: (i, 0))
    return pl.pallas_call(
        _inproj_kernel,
        grid=(T // tm,),
        in_specs=[row(D_MODEL), pl.BlockSpec(w_in.shape, lambda i: (0, 0)),
                  tab_spec, tab_spec, tab_spec, tab_spec],
        out_specs=[row(RET_WIDTH), row(RET_WIDTH), row(RET_WIDTH), row(RET_WIDTH),
                   row(SWA_WIDTH), row(8 * LANES)],
        out_shape=[jax.ShapeDtypeStruct((T, RET_WIDTH), BF16),
                   jax.ShapeDtypeStruct((T, RET_WIDTH), BF16),
                   jax.ShapeDtypeStruct((T, RET_WIDTH), BF16),
                   jax.ShapeDtypeStruct((T, RET_WIDTH), F32),
                   jax.ShapeDtypeStruct((T, SWA_WIDTH), BF16),
                   jax.ShapeDtypeStruct((T, 8 * LANES), BF16)],
        compiler_params=_cparams(("parallel",)),
        name="inproj_rotary",
    )(xt, w_in, *tabs)


def _retention_kernel(rq_ref, rk_ref, rv_ref, rg_ref, gnw_ref, dec_ref, xi_ref, zeta_ref, g_ref,
                      y_ref, state_ref):
    @pl.when(pl.program_id(0) == 0)
    def _():
        state_ref[...] = jnp.zeros_like(state_ref)

    nb = rq_ref.shape[0]
    C = RET_CHUNK
    for b in range(nb):
        for h in range(RET_HEADS):
            cs = slice(h * LANES, (h + 1) * LANES)
            st = state_ref[b, h]
            for c in range(rq_ref.shape[1] // C):
                rs = slice(c * C, (c + 1) * C)
                q = rq_ref[b, rs, cs]
                k = rk_ref[b, rs, cs]
                v = rv_ref[b, rs, cs]
                s = _nt_dot(q, k) * dec_ref[h]
                y = jnp.dot(s.astype(BF16), v, preferred_element_type=F32)
                y = y + xi_ref[h] * jnp.dot(q, st.astype(BF16), preferred_element_type=F32)
                kz = (k.astype(F32) * zeta_ref[h]).T.astype(BF16)
                st = st * g_ref[h] + jnp.dot(kz, v, preferred_element_type=F32)
                mu = jnp.mean(y, axis=-1, keepdims=True)
                d = y - mu
                var = jnp.mean(d * d, axis=-1, keepdims=True)
                yn = d * lax.rsqrt(var + GN_EPS)
                gate = rg_ref[b, rs, cs]
                out = (gate * jax.nn.sigmoid(gate)) * (yn * gnw_ref[:, cs])
                y_ref[b, rs, cs] = out.astype(BF16)
            state_ref[b, h] = st


def _retention(rq, rk, rv, rg, gnw, consts, batch, seq):
    tc = TC_RET
    r3 = lambda a: a.reshape(batch, seq, RET_WIDTH)
    blk = pl.BlockSpec((batch, tc, RET_WIDTH), lambda j: (0, j, 0))
    cspec = pl.BlockSpec((RET_HEADS, RET_CHUNK, RET_CHUNK), lambda j: (0, 0, 0))
    y = pl.pallas_call(
        _retention_kernel,
        grid=(seq // tc,),
        in_specs=[blk, blk, blk, blk, pl.BlockSpec((1, RET_WIDTH), lambda j: (0, 0)),
                  cspec, cspec, cspec, cspec],
        out_specs=blk,
        out_shape=jax.ShapeDtypeStruct((batch, seq, RET_WIDTH), BF16),
        scratch_shapes=[pltpu.VMEM((batch, RET_HEADS, RET_HEAD_DIM, RET_HEAD_DIM), F32)],
        compiler_params=_cparams(("arbitrary",)),
        name="retention",
    )(r3(rq), r3(rk), r3(rv), r3(rg), gnw, *consts)
    return y.reshape(batch * seq, RET_WIDTH)


def _retention_consts():
    H, C, d = RET_HEADS, RET_CHUNK, RET_HEAD_DIM
    log_g = jnp.log(1.0 - 2.0 ** (-5.0 - jnp.arange(H, dtype=F32)))
    idx = jnp.arange(C, dtype=F32)
    diff = idx[:, None] - idx[None, :]
    scale = d ** -0.5
    decay = jnp.where(diff[None] >= 0, jnp.exp(jnp.maximum(diff, 0.0)[None] * log_g[:, None, None]), 0.0)
    zeta = jnp.exp((C - 1.0 - idx)[None] * log_g[:, None])
    xi = jnp.exp((idx + 1.0)[None] * log_g[:, None])
    g_chunk = jnp.exp(C * log_g)
    bc = lambda a: jnp.broadcast_to(a[:, :, None], (H, C, C)).astype(F32)
    return (decay * scale).astype(F32), bc(xi), bc(zeta * scale), \
        jnp.broadcast_to(g_chunk[:, None, None], (H, C, C)).astype(F32)


def _swa_kernel(sinks_ref, q_ref, kv_ref, kvp_ref, o_ref, *, blocks_per_seq):
    W = WINDOW
    i = pl.program_id(0)
    first = (i % blocks_per_seq) == 0
    qi = lax.broadcasted_iota(jnp.int32, (W, W), 0)
    kj = lax.broadcasted_iota(jnp.int32, (W, W), 1)
    cur_ok = kj <= qi
    prev_ok = kj > qi
    lane_lo = lax.broadcasted_iota(jnp.int32, (W, LANES), 1) < 64
    nq = q_ref.shape[0] // W
    for j in range(nq):
        rs = slice(j * W, (j + 1) * W)
        for p in range(SWA_WIDTH // LANES):
            kvh = p // 2
            q = q_ref[rs, p * LANES:(p + 1) * LANES]
            o_pair = None
            inv = []
            for e in range(2):
                kc = slice((kvh * 2 + e) * LANES, (kvh * 2 + e + 1) * LANES)
                vc = slice((4 + kvh * 2 + e) * LANES, (4 + kvh * 2 + e + 1) * LANES)
                if j == 0:
                    k_prev, v_prev = kvp_ref[:, kc], kvp_ref[:, vc]
                    p_ok = jnp.logical_and(prev_ok, jnp.logical_not(first))
                else:
                    ps = slice((j - 1) * W, j * W)
                    k_prev, v_prev = kv_ref[ps, kc], kv_ref[ps, vc]
                    p_ok = prev_ok
                sink = sinks_ref[2 * p + e]
                s_prev = jnp.where(p_ok, _nt_dot(q, k_prev), NEG_INF)
                s_cur = jnp.where(cur_ok, _nt_dot(q, kv_ref[rs, kc]), NEG_INF)
                m = jnp.maximum(jnp.maximum(jnp.max(s_prev, axis=-1, keepdims=True),
                                            jnp.max(s_cur, axis=-1, keepdims=True)), sink)
                p_prev = jnp.exp(s_prev - m)
                p_cur = jnp.exp(s_cur - m)
                denom = (jnp.sum(p_prev, axis=-1, keepdims=True) + jnp.sum(p_cur, axis=-1, keepdims=True)
                         + jnp.exp(sink - m))
                o = (jnp.dot(p_prev.astype(BF16), v_prev, preferred_element_type=F32)
                     + jnp.dot(p_cur.astype(BF16), kv_ref[rs, vc], preferred_element_type=F32))
                o_pair = o if o_pair is None else o_pair + o
                inv.append(1.0 / denom)
            o_ref[rs, p * LANES:(p + 1) * LANES] = (o_pair * jnp.where(lane_lo, inv[0], inv[1])).astype(BF16)


def _swa(sq, kv8, sinks, seq):
    T = sq.shape[0]
    tq = TQ_SWA
    nb = tq // WINDOW
    return pl.pallas_call(
        functools.partial(_swa_kernel, blocks_per_seq=seq // tq),
        grid=(T // tq,),
        in_specs=[pl.BlockSpec(memory_space=pltpu.SMEM),
                  pl.BlockSpec((tq, SWA_WIDTH), lambda i: (i, 0)),
                  pl.BlockSpec((tq, 8 * LANES), lambda i: (i, 0)),
                  pl.BlockSpec((WINDOW, 8 * LANES), lambda i: (jnp.maximum(i * nb - 1, 0), 0))],
        out_specs=pl.BlockSpec((tq, SWA_WIDTH), lambda i: (i, 0)),
        out_shape=jax.ShapeDtypeStruct((T, SWA_WIDTH), BF16),
        compiler_params=_cparams(("parallel",)),
        name="swa",
    )(sinks, sq, kv8, kv8)


def _layer_norm(y, w, b):
    mu = jnp.mean(y, axis=-1, keepdims=True)
    d = y - mu
    var = jnp.mean(d * d, axis=-1, keepdims=True)
    return d * lax.rsqrt(var + LN_EPS) * w + b


def _outproj_kernel(yr_ref, ys_ref, x_ref, wo_ref, lnw_ref, lnb_ref, wpq_ref, keys_ref,
                    h_ref, ht_ref, st_ref):
    mix = (jnp.dot(yr_ref[...], wo_ref[:RET_WIDTH, :], preferred_element_type=F32)
           + jnp.dot(ys_ref[...], wo_ref[RET_WIDTH:, :], preferred_element_type=F32))
    h = _layer_norm(ALPHA * x_ref[...] + mix, lnw_ref[...], lnb_ref[...])
    h_ref[...] = h
    ht_ref[...] = pltpu.bitcast(h.T.astype(BF16), jnp.uint32)
    q = jnp.dot(h.astype(BF16), wpq_ref[...], preferred_element_type=F32).astype(BF16)
    for hp in range(2 * PEER_HEADS):
        st_ref[hp] = _nt_dot(keys_ref[hp], q[:, hp * PEER_HALF:(hp + 1) * PEER_HALF])


def _outproj(y_ret, y_swa, xt, w_out, ln_w, ln_b, w_pq, keys):
    T = xt.shape[0]
    tm = TM_OUT
    row = lambda w: pl.BlockSpec((tm, w), lambda i: (i, 0))
    full = lambda a: pl.BlockSpec(a.shape, lambda i: (0,) * a.ndim)
    return pl.pallas_call(
        _outproj_kernel,
        grid=(T // tm,),
        in_specs=[row(RET_WIDTH), row(SWA_WIDTH), row(D_MODEL), full(w_out), full(ln_w), full(ln_b),
                  full(w_pq), full(keys)],
        out_specs=[row(D_MODEL), pl.BlockSpec((D_MODEL // 2, tm), lambda i: (0, i)),
                   pl.BlockSpec((2 * PEER_HEADS, PEER_N_KEYS, tm), lambda i: (0, 0, i))],
        out_shape=[jax.ShapeDtypeStruct((T, D_MODEL), F32),
                   jax.ShapeDtypeStruct((D_MODEL // 2, T), jnp.uint32),
                   jax.ShapeDtypeStruct((2 * PEER_HEADS, PEER_N_KEYS, T), F32)],
        compiler_params=_cparams(("parallel",)),
        name="outproj_ln1_scores",
    )(y_ret, y_swa, xt, w_out, ln_w, ln_b, w_pq, keys)


def _top_rows(s, k):
    rows = []
    w = s
    rank = jnp.full(s.shape, float(k), F32)
    for a in range(k):
        m = jnp.max(w, axis=0, keepdims=True)
        rows.append(m)
        hit = w == m
        rank = jnp.where(hit, float(a), rank)
        if a + 1 < k:
            w = jnp.where(hit, -jnp.inf, w)
    return rows, rank


def _stack_rows(rows, lo, hi):
    sub = lax.broadcasted_iota(jnp.int32, (8, rows[0].shape[1]), 0)
    out = jnp.full((8, rows[0].shape[1]), -jnp.inf, F32)
    for r in range(lo, hi):
        out = jnp.where(sub == (r - lo), rows[r], out)
    return out


def _threshold_kernel(s_ref, cnt_ref, e1_ref, r2_ref, e2_ref):
    K = PEER_TOPK
    L = LANES

    @pl.loop(0, s_ref.shape[2] // L)
    def _(blk):
        ls = pl.ds(pl.multiple_of(blk * L, L), L)
        for h in range(PEER_HEADS):
            s1 = s_ref[2 * h, :, ls]
            s2 = s_ref[2 * h + 1, :, ls]
            t1, r1 = _top_rows(s1, K)
            t2, r2 = _top_rows(s2, K)
            t1_lo, t1_hi = _stack_rows(t1, 0, 8), _stack_rows(t1, 8, 16)
            t2_lo, t2_hi = _stack_rows(t2, 0, 8), _stack_rows(t2, 8, 16)
            sub = lax.broadcasted_iota(jnp.int32, (8, L), 0)
            cands = [t1_lo + t2[0], t1_hi + t2[0], t2_hi + t1[0]]
            for b in range(1, 8):
                a_max = K // (b + 1) - 1
                cands.append(jnp.where(sub <= a_max, t1_lo + t2[b], -jnp.inf))
            w = list(cands)
            tau = None
            for a in range(K):
                m = w[0]
                for c in w[1:]:
                    m = jnp.maximum(m, c)
                m = jnp.max(m, axis=0, keepdims=True)
                tau = m
                if a + 1 < K:
                    w = [jnp.where(c == m, -jnp.inf, c) for c in w]
            top = t1[0] + t2[0]
            z = None
            for c in cands:
                e = jnp.where(c >= tau, jnp.exp(c - top), 0.0)
                z = e if z is None else z + e
            inv_z = 1.0 / jnp.sum(z, axis=0, keepdims=True)
            r1b = r1.astype(BF16)
            tile = 16
            cnt = [jnp.zeros((tile, L), BF16)] * (PEER_N_KEYS // tile)
            for a in range(K):
                b_max = K // (a + 1) - 1
                n = jnp.where(jnp.logical_and(t1[a] + t2_lo >= tau, sub <= b_max), 1.0, 0.0)
                if a == 0:
                    n = n + jnp.where(t1[a] + t2_hi >= tau, 1.0, 0.0)
                n_a = jnp.broadcast_to(jnp.sum(n, axis=0, keepdims=True), (tile, L)).astype(BF16)
                cnt = [jnp.where(r1b[i * tile:(i + 1) * tile] == float(a), n_a, c) for i, c in enumerate(cnt)]
            cnt_ref[h, :, ls] = jnp.concatenate(cnt, axis=0).astype(F32)
            e1_ref[h, :, ls] = jnp.exp(s1 - t1[0]) * inv_z
            r2_ref[h, :, ls] = pltpu.bitcast(r2.astype(BF16), jnp.uint32)
            e2_ref[h, :, ls] = pltpu.bitcast(jnp.exp(s2 - t2[0]).astype(BF16), jnp.uint32)


def _thresholds(scores_t):
    T = scores_t.shape[2]
    te = TE_THR
    blk = pl.BlockSpec((PEER_HEADS, PEER_N_KEYS, te), lambda i: (0, 0, i))
    shp = jax.ShapeDtypeStruct((PEER_HEADS, PEER_N_KEYS, T), F32)
    pblk = pl.BlockSpec((PEER_HEADS, PEER_N_KEYS // 2, te), lambda i: (0, 0, i))
    pshp = jax.ShapeDtypeStruct((PEER_HEADS, PEER_N_KEYS // 2, T), jnp.uint32)
    return pl.pallas_call(
        _threshold_kernel,
        grid=(T // te,),
        in_specs=[pl.BlockSpec((2 * PEER_HEADS, PEER_N_KEYS, te), lambda i: (0, 0, i))],
        out_specs=[blk, blk, pblk, pblk],
        out_shape=[shp, shp, pshp, pshp],
        compiler_params=_cparams(("parallel",)),
        name="peer_threshold",
    )(scores_t)


def _gelu_exact(z):
    return 0.5 * z * (1.0 + lax.erf(z * (2.0 ** -0.5)))


def _peer_kernel(u_ref, ht_ref, vt_ref, cnt_ref, e1_ref, r2_ref, e2_ref, acc_ref, z_ref, a_ref):
    j = pl.program_id(1)
    tn = 2 * u_ref.shape[0]
    tm = ht_ref.shape[1]
    groups = tn // PEER_N_KEYS
    nblk = tm // TB_PEER
    SUB = 16
    tiles = PEER_N_KEYS // SUB
    zero = jnp.zeros((SUB, LANES), BF16)

    @pl.when(j == 0)
    def _():
        acc_ref[...] = jnp.zeros_like(acc_ref)

    def cols(blk):
        return slice(blk * TB_PEER, (blk + 1) * TB_PEER)

    def scores(blk):
        z_ref[blk % 2] = jnp.dot(pltpu.bitcast(u_ref[...], BF16), pltpu.bitcast(ht_ref[:, cols(blk)], BF16),
                                 preferred_element_type=F32)

    def activations(blk):
        slot = blk % 2
        for k in range(groups):
            for half in range(TB_PEER // LANES):
                ls = slice(blk * TB_PEER + half * LANES, blk * TB_PEER + (half + 1) * LANES)
                zs = slice(half * LANES, (half + 1) * LANES)
                gate = [zero] * tiles
                for h in range(PEER_HEADS):
                    cnt = jnp.broadcast_to(cnt_ref[h, k:k + 1, ls], (SUB, LANES)).astype(BF16)
                    e1 = jnp.broadcast_to(e1_ref[h, k:k + 1, ls], (SUB, LANES)).astype(BF16)
                    for rt in range(tiles):
                        pr = slice(rt * SUB // 2, (rt + 1) * SUB // 2)
                        r2 = pltpu.bitcast(r2_ref[h, pr, ls], BF16)
                        e2 = pltpu.bitcast(e2_ref[h, pr, ls], BF16)
                        gate[rt] = gate[rt] + jnp.where(r2 < cnt, e2, zero) * e1
                for rt in range(tiles):
                    rr = slice(k * PEER_N_KEYS + rt * SUB, k * PEER_N_KEYS + (rt + 1) * SUB)
                    pr = slice((k * PEER_N_KEYS + rt * SUB) // 2, (k * PEER_N_KEYS + (rt + 1) * SUB) // 2)
                    a = _gelu_exact(z_ref[slot, rr, zs]).astype(BF16) * gate[rt]
                    a_ref[slot, pr, zs] = pltpu.bitcast(a, jnp.uint32)

    def project(blk):
        acc_ref[:, cols(blk)] += jnp.dot(pltpu.bitcast(vt_ref[...], BF16),
                                         pltpu.bitcast(a_ref[blk % 2], BF16),
                                         preferred_element_type=F32)

    scores(0)
    for blk in range(nblk):
        if blk + 1 < nblk:
            scores(blk + 1)
        activations(blk)
        project(blk)


def _peer(u, ht, vt, cnt, e1, r2, e2):
    T = ht.shape[1]
    tm, tn = TM_PEER, TN_PEER
    groups = tn // PEER_N_KEYS
    rows = pl.BlockSpec((PEER_HEADS, groups, tm), lambda i, j: (0, j, i))
    keys = pl.BlockSpec((PEER_HEADS, PEER_N_KEYS // 2, tm), lambda i, j: (0, 0, i))
    return pl.pallas_call(
        _peer_kernel,
        grid=(T // tm, PEER_N_EXPERTS // tn),
        in_specs=[pl.BlockSpec((tn // 2, D_MODEL), lambda i, j: (j, 0)),
                  pl.BlockSpec((D_MODEL // 2, tm), lambda i, j: (0, i)),
                  pl.BlockSpec((D_MODEL // 2, tn), lambda i, j: (0, j)),
                  rows, rows, keys, keys],
        out_specs=pl.BlockSpec((D_MODEL, tm), lambda i, j: (0, i)),
        out_shape=jax.ShapeDtypeStruct((D_MODEL, T), F32),
        scratch_shapes=[pltpu.VMEM((2, tn, TB_PEER), F32),
                        pltpu.VMEM((2, tn // 2, TB_PEER), jnp.uint32)],
        compiler_params=_cparams(("parallel", "arbitrary")),
        name="peer_ffn",
    )(u, ht, vt, cnt, e1, r2, e2)


def _ln2_kernel(h_ref, ft_ref, lnw_ref, lnb_ref, out_ref):
    y = ALPHA * h_ref[...] + ft_ref[...].T
    out_ref[...] = _layer_norm(y, lnw_ref[...], lnb_ref[...])


def _ln2(h, ffn_t, ln_w, ln_b):
    T = h.shape[0]
    tm = TM_OUT
    vec = pl.BlockSpec((1, D_MODEL), lambda i: (0, 0))
    return pl.pallas_call(
        _ln2_kernel,
        grid=(T // tm,),
        in_specs=[pl.BlockSpec((tm, D_MODEL), lambda i: (i, 0)),
                  pl.BlockSpec((D_MODEL, tm), lambda i: (0, i)), vec, vec],
        out_specs=pl.BlockSpec((tm, D_MODEL), lambda i: (i, 0)),
        out_shape=jax.ShapeDtypeStruct((T, D_MODEL), F32),
        compiler_params=_cparams(("parallel",)),
        name="residual_ln2",
    )(h, ffn_t, ln_w, ln_b)


def _pack_tables_kernel(u_ref, v_ref, up_ref, vtp_ref):
    up_ref[...] = pltpu.bitcast(u_ref[...].astype(BF16), jnp.uint32)
    vtp_ref[...] = pltpu.bitcast(v_ref[...].T.astype(BF16), jnp.uint32)


def _pack_tables(u, v):
    n = u.shape[0]
    tp = 512
    return pl.pallas_call(
        _pack_tables_kernel,
        grid=(n // tp,),
        in_specs=[pl.BlockSpec((tp, D_MODEL), lambda i: (i, 0)), pl.BlockSpec((tp, D_MODEL), lambda i: (i, 0))],
        out_specs=[pl.BlockSpec((tp // 2, D_MODEL), lambda i: (i, 0)),
                   pl.BlockSpec((D_MODEL // 2, tp), lambda i: (0, i))],
        out_shape=[jax.ShapeDtypeStruct((n // 2, D_MODEL), jnp.uint32),
                   jax.ShapeDtypeStruct((D_MODEL // 2, n), jnp.uint32)],
        compiler_params=_cparams(("parallel",)),
        name="pack_expert_tables",
    )(u, v)


def _rotary_tables(seq):
    pos = jnp.arange(seq, dtype=F32)

    def tab(d):
        inv = 1.0 / (ROPE_THETA ** (jnp.arange(0, d, 2, dtype=F32) / d))
        ang = pos[:, None] * inv[None, :]
        cos, sin = jnp.cos(ang), jnp.sin(ang)
        reps = LANES // d
        c = jnp.tile(jnp.concatenate([cos, cos], axis=-1), (1, reps))
        s = jnp.tile(jnp.concatenate([-sin, sin], axis=-1), (1, reps))
        return c, s

    cr, sr = tab(RET_HEAD_DIM)
    cs, ss = tab(SWA_HEAD_DIM)
    return cr, sr, cs, ss


def kernel(x, w_in, ret_gn_w, swa_sinks, w_out, ln1_w, ln1_b, w_pq, peer_sub_keys, peer_u, peer_v, ln2_w, ln2_b):
    B, S, D = x.shape
    assert D == D_MODEL and w_in.shape[0] == DEPTH
    T = B * S
    h = x.reshape(T, D)
    tabs = _rotary_tables(S)
    ret_consts = _retention_consts()
    for l in range(DEPTH):
        rq, rk, rv, rg, sq, kv8 = _inproj(h, w_in[l].astype(BF16), tabs, S)
        y_ret = _retention(rq, rk, rv, rg, ret_gn_w[l].reshape(1, RET_WIDTH), ret_consts, B, S)
        y_swa = _swa(sq, kv8, swa_sinks[l], S)
        keys = peer_sub_keys[l].reshape(2 * PEER_HEADS, PEER_N_KEYS, PEER_HALF).astype(BF16)
        h1, h1_t, scores_t = _outproj(y_ret, y_swa, h, w_out[l].astype(BF16), ln1_w[l].reshape(1, D),
                                      ln1_b[l].reshape(1, D), w_pq[l].astype(BF16), keys)
        cnt, e1, r2, e2 = _thresholds(scores_t)
        u_p, vt_p = _pack_tables(peer_u[l], peer_v[l])
        ffn_t = _peer(u_p, h1_t, vt_p, cnt, e1, r2, e2)
        h = _ln2(h1, ffn_t, ln2_w[l].reshape(1, D), ln2_b[l].reshape(1, D))
    return h.reshape(B, S, D)
```

```python
import functools
import math

import jax
import jax.numpy as jnp
from jax import lax
from jax.experimental import pallas as pl
from jax.experimental.pallas import tpu as pltpu

F32 = jnp.float32
BF16 = jnp.bfloat16

D_MODEL = 1024
RET_HEADS = 4
RET_HEAD_DIM = 128
RET_WIDTH = RET_HEADS * RET_HEAD_DIM
RET_CHUNK = 128
SWA_Q_HEADS = 8
SWA_KV_HEADS = 2
SWA_HEAD_DIM = 64
SWA_WIDTH = SWA_Q_HEADS * SWA_HEAD_DIM
SWA_KV_WIDTH = SWA_KV_HEADS * SWA_HEAD_DIM
WINDOW = 128
ROPE_THETA = 10000.0
PEER_HEADS = 8
PEER_N_KEYS = 128
PEER_N_EXPERTS = PEER_N_KEYS * PEER_N_KEYS
PEER_HALF = 128
PEER_TOPK = 16
LN_EPS = 1e-5
GN_EPS = 1e-6
DEPTH = 1
ALPHA = (2.0 * DEPTH) ** 0.25
NEG_INF = -1e30

LANES = 128
VMEM_LIMIT = 58 * 1024 * 1024

TM_IN = 512
TC_RET = 256
TQ_SWA = 512
TM_OUT = 512
TE_THR = 512
TM_PEER = 2048
TN_PEER = 1024
TB_PEER = 256


def _cparams(sem, **kw):
    return pltpu.CompilerParams(dimension_semantics=sem, vmem_limit_bytes=VMEM_LIMIT, **kw)


def _nt_dot(a, b):
    return lax.dot_general(a, b, (((1,), (1,)), ((), ())), preferred_element_type=F32)


def _inproj_kernel(x_ref, w_ref, cr_ref, sr_ref, cs_ref, ss_ref,
                   rq_ref, rk_ref, rv_ref, rg_ref, sq_ref, kv_ref):
    xb = x_ref[...].astype(BF16)
    cr, sr = cr_ref[...], sr_ref[...]
    cs, ss = cs_ref[...], ss_ref[...]

    def proj(c0, width):
        return jnp.dot(xb, w_ref[:, c0:c0 + width], preferred_element_type=F32)

    def rot_ret(p):
        return p * cr + pltpu.roll(p, 64, 1) * sr

    lane = lax.broadcasted_iota(jnp.int32, (x_ref.shape[0], LANES), 1)
    first_half = (lane % 64) < 32
    lo = lane < 64

    def rot_swa(p):
        swapped = jnp.where(first_half, pltpu.roll(p, 96, 1), pltpu.roll(p, 32, 1))
        return p * cs + swapped * ss

    pq = proj(0, RET_WIDTH)
    pk = proj(RET_WIDTH, RET_WIDTH)
    for h in range(RET_HEADS):
        c = slice(h * LANES, (h + 1) * LANES)
        rq_ref[:, c] = rot_ret(pq[:, c]).astype(BF16)
        rk_ref[:, c] = rot_ret(pk[:, c]).astype(BF16)
    rv_ref[...] = proj(2 * RET_WIDTH, RET_WIDTH).astype(BF16)
    rg_ref[...] = proj(3 * RET_WIDTH, RET_WIDTH)

    base = 4 * RET_WIDTH
    psq = proj(base, SWA_WIDTH)
    scale = SWA_HEAD_DIM ** -0.5
    for p in range(SWA_WIDTH // LANES):
        c = slice(p * LANES, (p + 1) * LANES)
        sq_ref[:, c] = (rot_swa(psq[:, c]) * scale).astype(BF16)

    pkv = proj(base + SWA_WIDTH, 2 * SWA_KV_WIDTH)
    sk = rot_swa(pkv[:, :LANES])
    sv = pkv[:, LANES:]
    zero = jnp.zeros_like(sk)
    for t, a in enumerate((sk, sv)):
        a_sw = pltpu.roll(a, 64, 1)
        o = t * 4 * LANES
        kv_ref[:, o + 0 * LANES:o + 1 * LANES] = jnp.where(lo, a, zero).astype(BF16)
        kv_ref[:, o + 1 * LANES:o + 2 * LANES] = jnp.where(lo, zero, a_sw).astype(BF16)
        kv_ref[:, o + 2 * LANES:o + 3 * LANES] = jnp.where(lo, a_sw, zero).astype(BF16)
        kv_ref[:, o + 3 * LANES:o + 4 * LANES] = jnp.where(lo, zero, a).astype(BF16)


def _inproj(xt, w_in, tabs, seq):
    T = xt.shape[0]
    tm = TM_IN
    nseq = seq // tm
    tab_spec = pl.BlockSpec((tm, LANES), lambda i: (i % nseq, 0))
    row = lambda w: pl.BlockSpec((tm, w), lambda i: (i, 0))
    return pl.pallas_call(
        _inproj_kernel,
        grid=(T // tm,),
        in_specs=[row(D_MODEL), pl.BlockSpec(w_in.shape, lambda i: (0, 0)),
                  tab_spec, tab_spec, tab_spec, tab_spec],
        out_specs=[row(RET_WIDTH), row(RET_WIDTH), row(RET_WIDTH), row(RET_WIDTH),
                   row(SWA_WIDTH), row(8 * LANES)],
        out_shape=[jax.ShapeDtypeStruct((T, RET_WIDTH), BF16),
                   jax.ShapeDtypeStruct((T, RET_WIDTH), BF16),
                   jax.ShapeDtypeStruct((T, RET_WIDTH), BF16),
                   jax.ShapeDtypeStruct((T, RET_WIDTH), F32),
                   jax.ShapeDtypeStruct((T, SWA_WIDTH), BF16),
                   jax.ShapeDtypeStruct((T, 8 * LANES), BF16)],
        compiler_params=_cparams(("parallel",)),
        name="inproj_rotary",
    )(xt, w_in, *tabs)


def _retention_kernel(rq_ref, rk_ref, rv_ref, rg_ref, gnw_ref, dec_ref, xi_ref, zeta_ref, g_ref,
                      y_ref, state_ref):
    @pl.when(pl.program_id(0) == 0)
    def _():
        state_ref[...] = jnp.zeros_like(state_ref)

    nb = rq_ref.shape[0]
    C = RET_CHUNK
    for b in range(nb):
        for h in range(RET_HEADS):
            cs = slice(h * LANES, (h + 1) * LANES)
            st = state_ref[b, h]
            for c in range(rq_ref.shape[1] // C):
                rs = slice(c * C, (c + 1) * C)
                q = rq_ref[b, rs, cs]
                k = rk_ref[b, rs, cs]
                v = rv_ref[b, rs, cs]
                s = _nt_dot(q, k) * dec_ref[h]
                y = jnp.dot(s.astype(BF16), v, preferred_element_type=F32)
                y = y + xi_ref[h] * jnp.dot(q, st.astype(BF16), preferred_element_type=F32)
                kz = (k.astype(F32) * zeta_ref[h]).T.astype(BF16)
                st = st * g_ref[h] + jnp.dot(kz, v, preferred_element_type=F32)
                mu = jnp.mean(y, axis=-1, keepdims=True)
                d = y - mu
                var = jnp.mean(d * d, axis=-1, keepdims=True)
                yn = d * lax.rsqrt(var + GN_EPS)
                gate = rg_ref[b, rs, cs]
                out = (gate * jax.nn.sigmoid(gate)) * (yn * gnw_ref[:, cs])
                y_ref[b, rs, cs] = out.astype(BF16)
            state_ref[b, h] = st


def _retention(rq, rk, rv, rg, gnw, consts, batch, seq):
    tc = TC_RET
    r3 = lambda a: a.reshape(batch, seq, RET_WIDTH)
    blk = pl.BlockSpec((batch, tc, RET_WIDTH), lambda j: (0, j, 0))
    cspec = pl.BlockSpec((RET_HEADS, RET_CHUNK, RET_CHUNK), lambda j: (0, 0, 0))
    y = pl.pallas_call(
        _retention_kernel,
        grid=(seq // tc,),
        in_specs=[blk, blk, blk, blk, pl.BlockSpec((1, RET_WIDTH), lambda j: (0, 0)),
                  cspec, cspec, cspec, cspec],
        out_specs=blk,
        out_shape=jax.ShapeDtypeStruct((batch, seq, RET_WIDTH), BF16),
        scratch_shapes=[pltpu.VMEM((batch, RET_HEADS, RET_HEAD_DIM, RET_HEAD_DIM), F32)],
        compiler_params=_cparams(("arbitrary",)),
        name="retention",
    )(r3(rq), r3(rk), r3(rv), r3(rg), gnw, *consts)
    return y.reshape(batch * seq, RET_WIDTH)


def _retention_consts():
    H, C, d = RET_HEADS, RET_CHUNK, RET_HEAD_DIM
    log_g = jnp.log(1.0 - 2.0 ** (-5.0 - jnp.arange(H, dtype=F32)))
    idx = jnp.arange(C, dtype=F32)
    diff = idx[:, None] - idx[None, :]
    scale = d ** -0.5
    decay = jnp.where(diff[None] >= 0, jnp.exp(jnp.maximum(diff, 0.0)[None] * log_g[:, None, None]), 0.0)
    zeta = jnp.exp((C - 1.0 - idx)[None] * log_g[:, None])
    xi = jnp.exp((idx + 1.0)[None] * log_g[:, None])
    g_chunk = jnp.exp(C * log_g)
    bc = lambda a: jnp.broadcast_to(a[:, :, None], (H, C, C)).astype(F32)
    return (decay * scale).astype(F32), bc(xi), bc(zeta * scale), \
        jnp.broadcast_to(g_chunk[:, None, None], (H, C, C)).astype(F32)


def _swa_kernel(sinks_ref, q_ref, kv_ref, kvp_ref, o_ref, *, blocks_per_seq):
    W = WINDOW
    i = pl.program_id(0)
    first = (i % blocks_per_seq) == 0
    qi = lax.broadcasted_iota(jnp.int32, (W, W), 0)
    kj = lax.broadcasted_iota(jnp.int32, (W, W), 1)
    cur_ok = kj <= qi
    prev_ok = kj > qi
    lane_lo = lax.broadcasted_iota(jnp.int32, (W, LANES), 1) < 64
    nq = q_ref.shape[0] // W
    for j in range(nq):
        rs = slice(j * W, (j + 1) * W)
        for p in range(SWA_WIDTH // LANES):
            kvh = p // 2
            q = q_ref[rs, p * LANES:(p + 1) * LANES]
            o_pair = None
            inv = []
            for e in range(2):
                kc = slice((kvh * 2 + e) * LANES, (kvh * 2 + e + 1) * LANES)
                vc = slice((4 + kvh * 2 + e) * LANES, (4 + kvh * 2 + e + 1) * LANES)
                if j == 0:
                    k_prev, v_prev = kvp_ref[:, kc], kvp_ref[:, vc]
                    p_ok = jnp.logical_and(prev_ok, jnp.logical_not(first))
                else:
                    ps = slice((j - 1) * W, j * W)
                    k_prev, v_prev = kv_ref[ps, kc], kv_ref[ps, vc]
                    p_ok = prev_ok
                sink = sinks_ref[2 * p + e]
                s_prev = jnp.where(p_ok, _nt_dot(q, k_prev), NEG_INF)
                s_cur = jnp.where(cur_ok, _nt_dot(q, kv_ref[rs, kc]), NEG_INF)
                m = jnp.maximum(jnp.maximum(jnp.max(s_prev, axis=-1, keepdims=True),
                                            jnp.max(s_cur, axis=-1, keepdims=True)), sink)
                p_prev = jnp.exp(s_prev - m)
                p_cur = jnp.exp(s_cur - m)
                denom = (jnp.sum(p_prev, axis=-1, keepdims=True) + jnp.sum(p_cur, axis=-1, keepdims=True)
                         + jnp.exp(sink - m))
                o = (jnp.dot(p_prev.astype(BF16), v_prev, preferred_element_type=F32)
                     + jnp.dot(p_cur.astype(BF16), kv_ref[rs, vc], preferred_element_type=F32))
                o_pair = o if o_pair is None else o_pair + o
                inv.append(1.0 / denom)
            o_ref[rs, p * LANES:(p + 1) * LANES] = (o_pair * jnp.where(lane_lo, inv[0], inv[1])).astype(BF16)


def _swa(sq, kv8, sinks, seq):
    T = sq.shape[0]
    tq = TQ_SWA
    nb = tq // WINDOW
    return pl.pallas_call(
        functools.partial(_swa_kernel, blocks_per_seq=seq // tq),
        grid=(T // tq,),
        in_specs=[pl.BlockSpec(memory_space=pltpu.SMEM),
                  pl.BlockSpec((tq, SWA_WIDTH), lambda i: (i, 0)),
                  pl.BlockSpec((tq, 8 * LANES), lambda i: (i, 0)),
                  pl.BlockSpec((WINDOW, 8 * LANES), lambda i: (jnp.maximum(i * nb - 1, 0), 0))],
        out_specs=pl.BlockSpec((tq, SWA_WIDTH), lambda i: (i, 0)),
        out_shape=jax.ShapeDtypeStruct((T, SWA_WIDTH), BF16),
        compiler_params=_cparams(("parallel",)),
        name="swa",
    )(sinks, sq, kv8, kv8)


def _layer_norm(y, w, b):
    mu = jnp.mean(y, axis=-1, keepdims=True)
    d = y - mu
    var = jnp.mean(d * d, axis=-1, keepdims=True)
    return d * lax.rsqrt(var + LN_EPS) * w + b


def _outproj_kernel(yr_ref, ys_ref, x_ref, wo_ref, lnw_ref, lnb_ref, wpq_ref, keys_ref,
                    h_ref, ht_ref, st_ref):
    mix = (jnp.dot(yr_ref[...], wo_ref[:RET_WIDTH, :], preferred_element_type=F32)
           + jnp.dot(ys_ref[...], wo_ref[RET_WIDTH:, :], preferred_element_type=F32))
    h = _layer_norm(ALPHA * x_ref[...] + mix, lnw_ref[...], lnb_ref[...])
    h_ref[...] = h
    ht_ref[...] = pltpu.bitcast(h.T.astype(BF16), jnp.uint32)
    q = jnp.dot(h.astype(BF16), wpq_ref[...], preferred_element_type=F32).astype(BF16)
    for hp in range(2 * PEER_HEADS):
        st_ref[hp] = _nt_dot(keys_ref[hp], q[:, hp * PEER_HALF:(hp + 1) * PEER_HALF])


def _outproj(y_ret, y_swa, xt, w_out, ln_w, ln_b, w_pq, keys):
    T = xt.shape[0]
    tm = TM_OUT
    row = lambda w: pl.BlockSpec((tm, w), lambda i: (i, 0))
    full = lambda a: pl.BlockSpec(a.shape, lambda i: (0,) * a.ndim)
    return pl.pallas_call(
        _outproj_kernel,
        grid=(T // tm,),
        in_specs=[row(RET_WIDTH), row(SWA_WIDTH), row(D_MODEL), full(w_out), full(ln_w), full(ln_b),
                  full(w_pq), full(keys)],
        out_specs=[row(D_MODEL), pl.BlockSpec((D_MODEL // 2, tm), lambda i: (0, i)),
                   pl.BlockSpec((2 * PEER_HEADS, PEER_N_KEYS, tm), lambda i: (0, 0, i))],
        out_shape=[jax.ShapeDtypeStruct((T, D_MODEL), F32),
                   jax.ShapeDtypeStruct((D_MODEL // 2, T), jnp.uint32),
                   jax.ShapeDtypeStruct((2 * PEER_HEADS, PEER_N_KEYS, T), F32)],
        compiler_params=_cparams(("parallel",)),
        name="outproj_ln1_scores",
    )(y_ret, y_swa, xt, w_out, ln_w, ln_b, w_pq, keys)


def _odd_even_merge_sort_pairs(n):
    pairs = []
    p = 1
    while p < n:
        k = p
        while k >= 1:
            for j in range(k % p, n - k, 2 * k):
                for i in range(min(k, n - j - k)):
                    if (i + j) // (2 * p) == (i + j + k) // (2 * p):
                        pairs.append((i + j, i + j + k))
            k //= 2
        p *= 2
    return pairs


_SORT16 = _odd_even_merge_sort_pairs(16)


def _top_rows(s, k):
    sub = 8
    n = s.shape[0] // sub
    v = [s[i * sub:(i + 1) * sub] for i in range(n)]
    for i, j in _SORT16:
        v[i], v[j] = jnp.maximum(v[i], v[j]), jnp.minimum(v[i], v[j])
    rows = []
    for a in range(k):
        m = jnp.max(v[0], axis=0, keepdims=True)
        rows.append(m)
        hit = v[0] == m
        for i in range(k - 1 - a):
            v[i] = jnp.where(hit, v[i + 1], v[i])
    rank = jnp.zeros(s.shape, F32)
    for a in range(k):
        rank = jnp.where(rows[a] > s, float(a + 1), rank)
    return rows, rank


def _stack_rows(rows, lo, hi):
    sub = lax.broadcasted_iota(jnp.int32, (8, rows[0].shape[1]), 0)
    out = jnp.full((8, rows[0].shape[1]), -jnp.inf, F32)
    for r in range(lo, hi):
        out = jnp.where(sub == (r - lo), rows[r], out)
    return out


def _threshold_kernel(s_ref, cnt_ref, e1_ref, r2_ref, e2_ref):
    K = PEER_TOPK
    L = LANES

    @pl.loop(0, s_ref.shape[2] // L)
    def _(blk):
        ls = pl.ds(pl.multiple_of(blk * L, L), L)
        for h in range(PEER_HEADS):
            s1 = s_ref[2 * h, :, ls]
            s2 = s_ref[2 * h + 1, :, ls]
            t1, r1 = _top_rows(s1, K)
            t2, r2 = _top_rows(s2, K)
            t1_lo, t1_hi = _stack_rows(t1, 0, 8), _stack_rows(t1, 8, 16)
            t2_lo, t2_hi = _stack_rows(t2, 0, 8), _stack_rows(t2, 8, 16)
            sub = lax.broadcasted_iota(jnp.int32, (8, L), 0)
            cands = [t1_lo + t2[0], t1_hi + t2[0], t2_hi + t1[0]]
            for b in range(1, 8):
                a_max = K // (b + 1) - 1
                cands.append(jnp.where(sub <= a_max, t1_lo + t2[b], -jnp.inf))
            w = list(cands)
            tau = None
            for a in range(K):
                m = w[0]
                for c in w[1:]:
                    m = jnp.maximum(m, c)
                m = jnp.max(m, axis=0, keepdims=True)
                tau = m
                if a + 1 < K:
                    w = [jnp.where(c == m, -jnp.inf, c) for c in w]
            top = t1[0] + t2[0]
            z = None
            for c in cands:
                e = jnp.where(c >= tau, jnp.exp(c - top), 0.0)
                z = e if z is None else z + e
            inv_z = 1.0 / jnp.sum(z, axis=0, keepdims=True)
            r1b = r1.astype(BF16)
            tile = 16
            cnt = [jnp.zeros((tile, L), BF16)] * (PEER_N_KEYS // tile)
            for a in range(K):
                b_max = K // (a + 1) - 1
                n = jnp.where(jnp.logical_and(t1[a] + t2_lo >= tau, sub <= b_max), 1.0, 0.0)
                if a == 0:
                    n = n + jnp.where(t1[a] + t2_hi >= tau, 1.0, 0.0)
                n_a = jnp.broadcast_to(jnp.sum(n, axis=0, keepdims=True), (tile, L)).astype(BF16)
                cnt = [jnp.where(r1b[i * tile:(i + 1) * tile] == float(a), n_a, c) for i, c in enumerate(cnt)]
            cnt_ref[h, :, ls] = jnp.concatenate(cnt, axis=0).astype(F32)
            e1_ref[h, :, ls] = jnp.exp(s1 - t1[0]) * inv_z
            r2_ref[h, :, ls] = pltpu.bitcast(r2.astype(BF16), jnp.uint32)
            e2_ref[h, :, ls] = pltpu.bitcast(jnp.exp(s2 - t2[0]).astype(BF16), jnp.uint32)


def _thresholds(scores_t):
    T = scores_t.shape[2]
    te = TE_THR
    blk = pl.BlockSpec((PEER_HEADS, PEER_N_KEYS, te), lambda i: (0, 0, i))
    shp = jax.ShapeDtypeStruct((PEER_HEADS, PEER_N_KEYS, T), F32)
    pblk = pl.BlockSpec((PEER_HEADS, PEER_N_KEYS // 2, te), lambda i: (0, 0, i))
    pshp = jax.ShapeDtypeStruct((PEER_HEADS, PEER_N_KEYS // 2, T), jnp.uint32)
    return pl.pallas_call(
        _threshold_kernel,
        grid=(T // te,),
        in_specs=[pl.BlockSpec((2 * PEER_HEADS, PEER_N_KEYS, te), lambda i: (0, 0, i))],
        out_specs=[blk, blk, pblk, pblk],
        out_shape=[shp, shp, pshp, pshp],
        compiler_params=_cparams(("parallel",)),
        name="peer_threshold",
    )(scores_t)


def _gelu_exact(z):
    return 0.5 * z * (1.0 + lax.erf(z * (2.0 ** -0.5)))


def _peer_kernel(u_ref, ht_ref, vt_ref, cnt_ref, e1_ref, r2_ref, e2_ref, acc_ref, z_ref, a_ref):
    j = pl.program_id(1)
    tn = 2 * u_ref.shape[0]
    tm = ht_ref.shape[1]
    groups = tn // PEER_N_KEYS
    nblk = tm // TB_PEER
    SUB = 16
    tiles = PEER_N_KEYS // SUB
    zero = jnp.zeros((SUB, LANES), BF16)

    @pl.when(j == 0)
    def _():
        acc_ref[...] = jnp.zeros_like(acc_ref)

    def cols(blk):
        return slice(blk * TB_PEER, (blk + 1) * TB_PEER)

    def scores(blk):
        z_ref[blk % 2] = jnp.dot(pltpu.bitcast(u_ref[...], BF16), pltpu.bitcast(ht_ref[:, cols(blk)], BF16),
                                 preferred_element_type=F32)

    def activations(blk):
        slot = blk % 2
        KP = 2
        for k0 in range(0, groups, KP):
            for half in range(TB_PEER // LANES):
                ls = slice(blk * TB_PEER + half * LANES, blk * TB_PEER + (half + 1) * LANES)
                zs = slice(half * LANES, (half + 1) * LANES)
                gate = [[zero] * tiles for _ in range(KP)]
                for h in range(PEER_HEADS):
                    cnt = [jnp.broadcast_to(cnt_ref[h, k0 + d:k0 + d + 1, ls], (SUB, LANES)).astype(BF16)
                           for d in range(KP)]
                    e1 = [jnp.broadcast_to(e1_ref[h, k0 + d:k0 + d + 1, ls], (SUB, LANES)).astype(BF16)
                          for d in range(KP)]
                    for rt in range(tiles):
                        pr = slice(rt * SUB // 2, (rt + 1) * SUB // 2)
                        r2 = pltpu.bitcast(r2_ref[h, pr, ls], BF16)
                        e2 = pltpu.bitcast(e2_ref[h, pr, ls], BF16)
                        for d in range(KP):
                            gate[d][rt] = gate[d][rt] + jnp.where(r2 < cnt[d], e2, zero) * e1[d]
                for d in range(KP):
                    for rt in range(tiles):
                        row0 = (k0 + d) * PEER_N_KEYS + rt * SUB
                        rr = slice(row0, row0 + SUB)
                        pr = slice(row0 // 2, (row0 + SUB) // 2)
                        a = _gelu_exact(z_ref[slot, rr, zs]).astype(BF16) * gate[d][rt]
                        a_ref[slot, pr, zs] = pltpu.bitcast(a, jnp.uint32)

    def project(blk):
        acc_ref[:, cols(blk)] += jnp.dot(pltpu.bitcast(vt_ref[...], BF16),
                                         pltpu.bitcast(a_ref[blk % 2], BF16),
                                         preferred_element_type=F32)

    scores(0)
    for blk in range(nblk):
        if blk + 1 < nblk:
            scores(blk + 1)
        activations(blk)
        project(blk)


def _peer(u, ht, vt, cnt, e1, r2, e2):
    T = ht.shape[1]
    tm, tn = TM_PEER, TN_PEER
    groups = tn // PEER_N_KEYS
    rows = pl.BlockSpec((PEER_HEADS, groups, tm), lambda i, j: (0, j, i))
    keys = pl.BlockSpec((PEER_HEADS, PEER_N_KEYS // 2, tm), lambda i, j: (0, 0, i))
    return pl.pallas_call(
        _peer_kernel,
        grid=(T // tm, PEER_N_EXPERTS // tn),
        in_specs=[pl.BlockSpec((tn // 2, D_MODEL), lambda i, j: (j, 0)),
                  pl.BlockSpec((D_MODEL // 2, tm), lambda i, j: (0, i)),
                  pl.BlockSpec((D_MODEL // 2, tn), lambda i, j: (0, j)),
                  rows, rows, keys, keys],
        out_specs=pl.BlockSpec((D_MODEL, tm), lambda i, j: (0, i)),
        out_shape=jax.ShapeDtypeStruct((D_MODEL, T), F32),
        scratch_shapes=[pltpu.VMEM((2, tn, TB_PEER), F32),
                        pltpu.VMEM((2, tn // 2, TB_PEER), jnp.uint32)],
        compiler_params=_cparams(("parallel", "arbitrary")),
        name="peer_ffn",
    )(u, ht, vt, cnt, e1, r2, e2)


def _ln2_kernel(h_ref, ft_ref, lnw_ref, lnb_ref, out_ref):
    y = ALPHA * h_ref[...] + ft_ref[...].T
    out_ref[...] = _layer_norm(y, lnw_ref[...], lnb_ref[...])


def _ln2(h, ffn_t, ln_w, ln_b):
    T = h.shape[0]
    tm = TM_OUT
    vec = pl.BlockSpec((1, D_MODEL), lambda i: (0, 0))
    return pl.pallas_call(
        _ln2_kernel,
        grid=(T // tm,),
        in_specs=[pl.BlockSpec((tm, D_MODEL), lambda i: (i, 0)),
                  pl.BlockSpec((D_MODEL, tm), lambda i: (0, i)), vec, vec],
        out_specs=pl.BlockSpec((tm, D_MODEL), lambda i: (i, 0)),
        out_shape=jax.ShapeDtypeStruct((T, D_MODEL), F32),
        compiler_params=_cparams(("parallel",)),
        name="residual_ln2",
    )(h, ffn_t, ln_w, ln_b)


def _pack_tables_kernel(u_ref, v_ref, up_ref, vtp_ref):
    up_ref[...] = pltpu.bitcast(u_ref[...].astype(BF16), jnp.uint32)
    vtp_ref[...] = pltpu.bitcast(v_ref[...].T.astype(BF16), jnp.uint32)


def _pack_tables(u, v):
    n = u.shape[0]
    tp = 512
    return pl.pallas_call(
        _pack_tables_kernel,
        grid=(n // tp,),
        in_specs=[pl.BlockSpec((tp, D_MODEL), lambda i: (i, 0)), pl.BlockSpec((tp, D_MODEL), lambda i: (i, 0))],
        out_specs=[pl.BlockSpec((tp // 2, D_MODEL), lambda i: (i, 0)),
                   pl.BlockSpec((D_MODEL // 2, tp), lambda i: (0, i))],
        out_shape=[jax.ShapeDtypeStruct((n // 2, D_MODEL), jnp.uint32),
                   jax.ShapeDtypeStruct((D_MODEL // 2, n), jnp.uint32)],
        compiler_params=_cparams(("parallel",)),
        name="pack_expert_tables",
    )(u, v)


def _rotary_tables(seq):
    pos = jnp.arange(seq, dtype=F32)

    def tab(d):
        inv = 1.0 / (ROPE_THETA ** (jnp.arange(0, d, 2, dtype=F32) / d))
        ang = pos[:, None] * inv[None, :]
        cos, sin = jnp.cos(ang), jnp.sin(ang)
        reps = LANES // d
        c = jnp.tile(jnp.concatenate([cos, cos], axis=-1), (1, reps))
        s = jnp.tile(jnp.concatenate([-sin, sin], axis=-1), (1, reps))
        return c, s

    cr, sr = tab(RET_HEAD_DIM)
    cs, ss = tab(SWA_HEAD_DIM)
    return cr, sr, cs, ss


def kernel(x, w_in, ret_gn_w, swa_sinks, w_out, ln1_w, ln1_b, w_pq, peer_sub_keys, peer_u, peer_v, ln2_w, ln2_b):
    B, S, D = x.shape
    assert D == D_MODEL and w_in.shape[0] == DEPTH
    T = B * S
    h = x.reshape(T, D)
    tabs = _rotary_tables(S)
    ret_consts = _retention_consts()
    for l in range(DEPTH):
        rq, rk, rv, rg, sq, kv8 = _inproj(h, w_in[l].astype(BF16), tabs, S)
        y_ret = _retention(rq, rk, rv, rg, ret_gn_w[l].reshape(1, RET_WIDTH), ret_consts, B, S)
        y_swa = _swa(sq, kv8, swa_sinks[l], S)
        keys = peer_sub_keys[l].reshape(2 * PEER_HEADS, PEER_N_KEYS, PEER_HALF).astype(BF16)
        h1, h1_t, scores_t = _outproj(y_ret, y_swa, h, w_out[l].astype(BF16), ln1_w[l].reshape(1, D),
                                      ln1_b[l].reshape(1, D), w_pq[l].astype(BF16), keys)
        cnt, e1, r2, e2 = _thresholds(scores_t)
        u_p, vt_p = _pack_tables(peer_u[l], peer_v[l])
        ffn_t = _peer(u_p, h1_t, vt_p, cnt, e1, r2, e2)
        h = _ln2(h1, ffn_t, ln2_w[l].reshape(1, D), ln2_b[l].reshape(1, D))
    return h.reshape(B, S, D)
```

```python
import functools
import math

import jax
import jax.numpy as jnp
from jax import lax
from jax.experimental import pallas as pl
from jax.experimental.pallas import tpu as pltpu

F32 = jnp.float32
BF16 = jnp.bfloat16

D_MODEL = 1024
RET_HEADS = 4
RET_HEAD_DIM = 128
RET_WIDTH = RET_HEADS * RET_HEAD_DIM
RET_CHUNK = 128
SWA_Q_HEADS = 8
SWA_KV_HEADS = 2
SWA_HEAD_DIM = 64
SWA_WIDTH = SWA_Q_HEADS * SWA_HEAD_DIM
SWA_KV_WIDTH = SWA_KV_HEADS * SWA_HEAD_DIM
WINDOW = 128
ROPE_THETA = 10000.0
PEER_HEADS = 8
PEER_N_KEYS = 128
PEER_N_EXPERTS = PEER_N_KEYS * PEER_N_KEYS
PEER_HALF = 128
PEER_TOPK = 16
LN_EPS = 1e-5
GN_EPS = 1e-6
DEPTH = 1
ALPHA = (2.0 * DEPTH) ** 0.25
NEG_INF = -1e30

LANES = 128
VMEM_LIMIT = 58 * 1024 * 1024

TM_IN = 512
TC_RET = 256
TQ_SWA = 512
TM_OUT = 512
TE_THR = 512
TM_PEER = 2048
TN_PEER = 1024
TB_PEER = 256


def _cparams(sem, **kw):
    return pltpu.CompilerParams(dimension_semantics=sem, vmem_limit_bytes=VMEM_LIMIT, **kw)


def _nt_dot(a, b):
    return lax.dot_general(a, b, (((1,), (1,)), ((), ())), preferred_element_type=F32)


def _inproj_kernel(x_ref, w_ref, cr_ref, sr_ref, cs_ref, ss_ref,
                   rq_ref, rk_ref, rv_ref, rg_ref, sq_ref, kv_ref):
    xb = x_ref[...].astype(BF16)
    cr, sr = cr_ref[...], sr_ref[...]
    cs, ss = cs_ref[...], ss_ref[...]

    def proj(c0, width):
        return jnp.dot(xb, w_ref[:, c0:c0 + width], preferred_element_type=F32)

    def rot_ret(p):
        return p * cr + pltpu.roll(p, 64, 1) * sr

    lane = lax.broadcasted_iota(jnp.int32, (x_ref.shape[0], LANES), 1)
    first_half = (lane % 64) < 32
    lo = lane < 64

    def rot_swa(p):
        swapped = jnp.where(first_half, pltpu.roll(p, 96, 1), pltpu.roll(p, 32, 1))
        return p * cs + swapped * ss

    pq = proj(0, RET_WIDTH)
    pk = proj(RET_WIDTH, RET_WIDTH)
    for h in range(RET_HEADS):
        c = slice(h * LANES, (h + 1) * LANES)
        rq_ref[:, c] = rot_ret(pq[:, c]).astype(BF16)
        rk_ref[:, c] = rot_ret(pk[:, c]).astype(BF16)
    rv_ref[...] = proj(2 * RET_WIDTH, RET_WIDTH).astype(BF16)
    rg_ref[...] = proj(3 * RET_WIDTH, RET_WIDTH)

    base = 4 * RET_WIDTH
    psq = proj(base, SWA_WIDTH)
    scale = SWA_HEAD_DIM ** -0.5
    for p in range(SWA_WIDTH // LANES):
        c = slice(p * LANES, (p + 1) * LANES)
        sq_ref[:, c] = (rot_swa(psq[:, c]) * scale).astype(BF16)

    pkv = proj(base + SWA_WIDTH, 2 * SWA_KV_WIDTH)
    sk = rot_swa(pkv[:, :LANES])
    sv = pkv[:, LANES:]
    zero = jnp.zeros_like(sk)
    for t, a in enumerate((sk, sv)):
        a_sw = pltpu.roll(a, 64, 1)
        o = t * 4 * LANES
        kv_ref[:, o + 0 * LANES:o + 1 * LANES] = jnp.where(lo, a, zero).astype(BF16)
        kv_ref[:, o + 1 * LANES:o + 2 * LANES] = jnp.where(lo, zero, a_sw).astype(BF16)
        kv_ref[:, o + 2 * LANES:o + 3 * LANES] = jnp.where(lo, a_sw, zero).astype(BF16)
        kv_ref[:, o + 3 * LANES:o + 4 * LANES] = jnp.where(lo, zero, a).astype(BF16)


def _inproj(xt, w_in, tabs, seq):
    T = xt.shape[0]
    tm = TM_IN
    nseq = seq // tm
    tab_spec = pl.BlockSpec((tm, LANES), lambda i: (i % nseq, 0))
    row = lambda w: pl.BlockSpec((tm, w), lambda i: (i, 0))
    return pl.pallas_call(
        _inproj_kernel,
        grid=(T // tm,),
        in_specs=[row(D_MODEL), pl.BlockSpec(w_in.shape, lambda i: (0, 0)),
                  tab_spec, tab_spec, tab_spec, tab_spec],
        out_specs=[row(RET_WIDTH), row(RET_WIDTH), row(RET_WIDTH), row(RET_WIDTH),
                   row(SWA_WIDTH), row(8 * LANES)],
        out_shape=[jax.ShapeDtypeStruct((T, RET_WIDTH), BF16),
                   jax.ShapeDtypeStruct((T, RET_WIDTH), BF16),
                   jax.ShapeDtypeStruct((T, RET_WIDTH), BF16),
                   jax.ShapeDtypeStruct((T, RET_WIDTH), F32),
                   jax.ShapeDtypeStruct((T, SWA_WIDTH), BF16),
                   jax.ShapeDtypeStruct((T, 8 * LANES), BF16)],
        compiler_params=_cparams(("parallel",)),
        name="inproj_rotary",
    )(xt, w_in, *tabs)


def _retention_kernel(rq_ref, rk_ref, rv_ref, rg_ref, gnw_ref, dec_ref, xi_ref, zeta_ref, g_ref,
                      y_ref, state_ref):
    @pl.when(pl.program_id(0) == 0)
    def _():
        state_ref[...] = jnp.zeros_like(state_ref)

    nb = rq_ref.shape[0]
    C = RET_CHUNK
    for b in range(nb):
        for h in range(RET_HEADS):
            cs = slice(h * LANES, (h + 1) * LANES)
            st = state_ref[b, h]
            for c in range(rq_ref.shape[1] // C):
                rs = slice(c * C, (c + 1) * C)
                q = rq_ref[b, rs, cs]
                k = rk_ref[b, rs, cs]
                v = rv_ref[b, rs, cs]
                s = _nt_dot(q, k) * dec_ref[h]
                y = jnp.dot(s.astype(BF16), v, preferred_element_type=F32)
                y = y + xi_ref[h] * jnp.dot(q, st.astype(BF16), preferred_element_type=F32)
                kz = (k.astype(F32) * zeta_ref[h]).T.astype(BF16)
                st = st * g_ref[h] + jnp.dot(kz, v, preferred_element_type=F32)
                mu = jnp.mean(y, axis=-1, keepdims=True)
                d = y - mu
                var = jnp.mean(d * d, axis=-1, keepdims=True)
                yn = d * lax.rsqrt(var + GN_EPS)
                gate = rg_ref[b, rs, cs]
                out = (gate * jax.nn.sigmoid(gate)) * (yn * gnw_ref[:, cs])
                y_ref[b, rs, cs] = out.astype(BF16)
            state_ref[b, h] = st


def _retention(rq, rk, rv, rg, gnw, consts, batch, seq):
    tc = TC_RET
    r3 = lambda a: a.reshape(batch, seq, RET_WIDTH)
    blk = pl.BlockSpec((batch, tc, RET_WIDTH), lambda j: (0, j, 0))
    cspec = pl.BlockSpec((RET_HEADS, RET_CHUNK, RET_CHUNK), lambda j: (0, 0, 0))
    y = pl.pallas_call(
        _retention_kernel,
        grid=(seq // tc,),
        in_specs=[blk, blk, blk, blk, pl.BlockSpec((1, RET_WIDTH), lambda j: (0, 0)),
                  cspec, cspec, cspec, cspec],
        out_specs=blk,
        out_shape=jax.ShapeDtypeStruct((batch, seq, RET_WIDTH), BF16),
        scratch_shapes=[pltpu.VMEM((batch, RET_HEADS, RET_HEAD_DIM, RET_HEAD_DIM), F32)],
        compiler_params=_cparams(("arbitrary",)),
        name="retention",
    )(r3(rq), r3(rk), r3(rv), r3(rg), gnw, *consts)
    return y.reshape(batch * seq, RET_WIDTH)


def _retention_consts():
    H, C, d = RET_HEADS, RET_CHUNK, RET_HEAD_DIM
    log_g = jnp.log(1.0 - 2.0 ** (-5.0 - jnp.arange(H, dtype=F32)))
    idx = jnp.arange(C, dtype=F32)
    diff = idx[:, None] - idx[None, :]
    scale = d ** -0.5
    decay = jnp.where(diff[None] >= 0, jnp.exp(jnp.maximum(diff, 0.0)[None] * log_g[:, None, None]), 0.0)
    zeta = jnp.exp((C - 1.0 - idx)[None] * log_g[:, None])
    xi = jnp.exp((idx + 1.0)[None] * log_g[:, None])
    g_chunk = jnp.exp(C * log_g)
    bc = lambda a: jnp.broadcast_to(a[:, :, None], (H, C, C)).astype(F32)
    return (decay * scale).astype(F32), bc(xi), bc(zeta * scale), \
        jnp.broadcast_to(g_chunk[:, None, None], (H, C, C)).astype(F32)


def _swa_kernel(sinks_ref, q_ref, kv_ref, kvp_ref, o_ref, *, blocks_per_seq):
    W = WINDOW
    i = pl.program_id(0)
    first = (i % blocks_per_seq) == 0
    qi = lax.broadcasted_iota(jnp.int32, (W, W), 0)
    kj = lax.broadcasted_iota(jnp.int32, (W, W), 1)
    cur_ok = kj <= qi
    prev_ok = kj > qi
    lane_lo = lax.broadcasted_iota(jnp.int32, (W, LANES), 1) < 64
    nq = q_ref.shape[0] // W
    for j in range(nq):
        rs = slice(j * W, (j + 1) * W)
        for p in range(SWA_WIDTH // LANES):
            kvh = p // 2
            q = q_ref[rs, p * LANES:(p + 1) * LANES]
            o_pair = None
            inv = []
            for e in range(2):
                kc = slice((kvh * 2 + e) * LANES, (kvh * 2 + e + 1) * LANES)
                vc = slice((4 + kvh * 2 + e) * LANES, (4 + kvh * 2 + e + 1) * LANES)
                if j == 0:
                    k_prev, v_prev = kvp_ref[:, kc], kvp_ref[:, vc]
                    p_ok = jnp.logical_and(prev_ok, jnp.logical_not(first))
                else:
                    ps = slice((j - 1) * W, j * W)
                    k_prev, v_prev = kv_ref[ps, kc], kv_ref[ps, vc]
                    p_ok = prev_ok
                sink = sinks_ref[2 * p + e]
                s_prev = jnp.where(p_ok, _nt_dot(q, k_prev), NEG_INF)
                s_cur = jnp.where(cur_ok, _nt_dot(q, kv_ref[rs, kc]), NEG_INF)
                m = jnp.maximum(jnp.maximum(jnp.max(s_prev, axis=-1, keepdims=True),
                                            jnp.max(s_cur, axis=-1, keepdims=True)), sink)
                p_prev = jnp.exp(s_prev - m)
                p_cur = jnp.exp(s_cur - m)
                denom = (jnp.sum(p_prev, axis=-1, keepdims=True) + jnp.sum(p_cur, axis=-1, keepdims=True)
                         + jnp.exp(sink - m))
                o = (jnp.dot(p_prev.astype(BF16), v_prev, preferred_element_type=F32)
                     + jnp.dot(p_cur.astype(BF16), kv_ref[rs, vc], preferred_element_type=F32))
                o_pair = o if o_pair is None else o_pair + o
                inv.append(1.0 / denom)
            o_ref[rs, p * LANES:(p + 1) * LANES] = (o_pair * jnp.where(lane_lo, inv[0], inv[1])).astype(BF16)


def _swa(sq, kv8, sinks, seq):
    T = sq.shape[0]
    tq = TQ_SWA
    nb = tq // WINDOW
    return pl.pallas_call(
        functools.partial(_swa_kernel, blocks_per_seq=seq // tq),
        grid=(T // tq,),
        in_specs=[pl.BlockSpec(memory_space=pltpu.SMEM),
                  pl.BlockSpec((tq, SWA_WIDTH), lambda i: (i, 0)),
                  pl.BlockSpec((tq, 8 * LANES), lambda i: (i, 0)),
                  pl.BlockSpec((WINDOW, 8 * LANES), lambda i: (jnp.maximum(i * nb - 1, 0), 0))],
        out_specs=pl.BlockSpec((tq, SWA_WIDTH), lambda i: (i, 0)),
        out_shape=jax.ShapeDtypeStruct((T, SWA_WIDTH), BF16),
        compiler_params=_cparams(("parallel",)),
        name="swa",
    )(sinks, sq, kv8, kv8)


def _layer_norm(y, w, b):
    mu = jnp.mean(y, axis=-1, keepdims=True)
    d = y - mu
    var = jnp.mean(d * d, axis=-1, keepdims=True)
    return d * lax.rsqrt(var + LN_EPS) * w + b


def _outproj_kernel(yr_ref, ys_ref, x_ref, wo_ref, lnw_ref, lnb_ref, wpq_ref, keys_ref,
                    h_ref, ht_ref, st_ref):
    mix = (jnp.dot(yr_ref[...], wo_ref[:RET_WIDTH, :], preferred_element_type=F32)
           + jnp.dot(ys_ref[...], wo_ref[RET_WIDTH:, :], preferred_element_type=F32))
    h = _layer_norm(ALPHA * x_ref[...] + mix, lnw_ref[...], lnb_ref[...])
    h_ref[...] = h
    ht_ref[...] = pltpu.bitcast(h.T.astype(BF16), jnp.uint32)
    q = jnp.dot(h.astype(BF16), wpq_ref[...], preferred_element_type=F32).astype(BF16)
    for hp in range(2 * PEER_HEADS):
        st_ref[hp] = _nt_dot(keys_ref[hp], q[:, hp * PEER_HALF:(hp + 1) * PEER_HALF])


def _outproj(y_ret, y_swa, xt, w_out, ln_w, ln_b, w_pq, keys):
    T = xt.shape[0]
    tm = TM_OUT
    row = lambda w: pl.BlockSpec((tm, w), lambda i: (i, 0))
    full = lambda a: pl.BlockSpec(a.shape, lambda i: (0,) * a.ndim)
    return pl.pallas_call(
        _outproj_kernel,
        grid=(T // tm,),
        in_specs=[row(RET_WIDTH), row(SWA_WIDTH), row(D_MODEL), full(w_out), full(ln_w), full(ln_b),
                  full(w_pq), full(keys)],
        out_specs=[row(D_MODEL), pl.BlockSpec((D_MODEL // 2, tm), lambda i: (0, i)),
                   pl.BlockSpec((2 * PEER_HEADS, PEER_N_KEYS, tm), lambda i: (0, 0, i))],
        out_shape=[jax.ShapeDtypeStruct((T, D_MODEL), F32),
                   jax.ShapeDtypeStruct((D_MODEL // 2, T), jnp.uint32),
                   jax.ShapeDtypeStruct((2 * PEER_HEADS, PEER_N_KEYS, T), F32)],
        compiler_params=_cparams(("parallel",)),
        name="outproj_ln1_scores",
    )(y_ret, y_swa, xt, w_out, ln_w, ln_b, w_pq, keys)


def _odd_even_merge_sort_pairs(n):
    pairs = []
    p = 1
    while p < n:
        k = p
        while k >= 1:
            for j in range(k % p, n - k, 2 * k):
                for i in range(min(k, n - j - k)):
                    if (i + j) // (2 * p) == (i + j + k) // (2 * p):
                        pairs.append((i + j, i + j + k))
            k //= 2
        p *= 2
    return pairs


_SORT16 = _odd_even_merge_sort_pairs(16)


def _top_rows(s, k, want_rank):
    sub = 8
    n = s.shape[0] // sub
    v = [s[i * sub:(i + 1) * sub] for i in range(n)]
    for i, j in _SORT16:
        v[i], v[j] = jnp.maximum(v[i], v[j]), jnp.minimum(v[i], v[j])
    rows = []
    for a in range(k):
        m = jnp.max(v[0], axis=0, keepdims=True)
        rows.append(m)
        hit = v[0] == m
        for i in range(k - 1 - a):
            v[i] = jnp.where(hit, v[i + 1], v[i])
    if not want_rank:
        return rows, None
    rank = jnp.zeros(s.shape, F32)
    for a in range(k):
        rank = jnp.where(rows[a] > s, float(a + 1), rank)
    return rows, rank


def _stack_rows(rows, lo, hi):
    sub = lax.broadcasted_iota(jnp.int32, (8, rows[0].shape[1]), 0)
    out = jnp.full((8, rows[0].shape[1]), -jnp.inf, F32)
    for r in range(lo, hi):
        out = jnp.where(sub == (r - lo), rows[r], out)
    return out


def _threshold_kernel(s_ref, cnt_ref, e1_ref, r2_ref, e2_ref):
    K = PEER_TOPK
    L = LANES

    @pl.loop(0, s_ref.shape[2] // L)
    def _(blk):
        ls = pl.ds(pl.multiple_of(blk * L, L), L)
        for h in range(PEER_HEADS):
            s1 = s_ref[2 * h, :, ls]
            s2 = s_ref[2 * h + 1, :, ls]
            t1, _ = _top_rows(s1, K, False)
            t2, r2 = _top_rows(s2, K, True)
            t1_lo, t1_hi = _stack_rows(t1, 0, 8), _stack_rows(t1, 8, 16)
            t2_hi = _stack_rows(t2, 8, 16)
            sub = lax.broadcasted_iota(jnp.int32, (8, L), 0)
            cands = [t1_lo + t2[0], t1_hi + t2[0], t2_hi + t1[0]]
            for b in range(1, 8):
                a_max = K // (b + 1) - 1
                cands.append(jnp.where(sub <= a_max, t1_lo + t2[b], -jnp.inf))
            w = list(cands)
            tau = None
            for a in range(K):
                m = w[0]
                for c in w[1:]:
                    m = jnp.maximum(m, c)
                m = jnp.max(m, axis=0, keepdims=True)
                tau = m
                if a + 1 < K:
                    w = [jnp.where(c == m, -jnp.inf, c) for c in w]
            top = t1[0] + t2[0]
            z = None
            for c in cands:
                e = jnp.where(c >= tau, jnp.exp(c - top), 0.0)
                z = e if z is None else z + e
            inv_z = 1.0 / jnp.sum(z, axis=0, keepdims=True)
            cnt = jnp.zeros(s1.shape, F32)
            for b in range(8):
                cnt = jnp.where(s1 + t2[b] >= tau, float(b + 1), cnt)
            n_hi = jnp.sum(jnp.where(t1[0] + t2_hi >= tau, 1.0, 0.0), axis=0, keepdims=True)
            cnt_ref[h, :, ls] = cnt + jnp.where(s1 == t1[0], n_hi, 0.0)
            e1_ref[h, :, ls] = jnp.exp(s1 - t1[0]) * inv_z
            r2_ref[h, :, ls] = pltpu.bitcast(r2.astype(BF16), jnp.uint32)
            e2_ref[h, :, ls] = pltpu.bitcast(jnp.exp(s2 - t2[0]).astype(BF16), jnp.uint32)


def _thresholds(scores_t):
    T = scores_t.shape[2]
    te = TE_THR
    blk = pl.BlockSpec((PEER_HEADS, PEER_N_KEYS, te), lambda i: (0, 0, i))
    shp = jax.ShapeDtypeStruct((PEER_HEADS, PEER_N_KEYS, T), F32)
    pblk = pl.BlockSpec((PEER_HEADS, PEER_N_KEYS // 2, te), lambda i: (0, 0, i))
    pshp = jax.ShapeDtypeStruct((PEER_HEADS, PEER_N_KEYS // 2, T), jnp.uint32)
    return pl.pallas_call(
        _threshold_kernel,
        grid=(T // te,),
        in_specs=[pl.BlockSpec((2 * PEER_HEADS, PEER_N_KEYS, te), lambda i: (0, 0, i))],
        out_specs=[blk, blk, pblk, pblk],
        out_shape=[shp, shp, pshp, pshp],
        compiler_params=_cparams(("parallel",)),
        name="peer_threshold",
    )(scores_t)


def _gelu_exact(z):
    return 0.5 * z * (1.0 + lax.erf(z * (2.0 ** -0.5)))


def _peer_kernel(u_ref, ht_ref, vt_ref, cnt_ref, e1_ref, r2_ref, e2_ref, acc_ref, z_ref, a_ref):
    j = pl.program_id(1)
    tn = 2 * u_ref.shape[0]
    tm = ht_ref.shape[1]
    groups = tn // PEER_N_KEYS
    nblk = tm // TB_PEER
    SUB = 16
    tiles = PEER_N_KEYS // SUB
    zero = jnp.zeros((SUB, LANES), BF16)

    @pl.when(j == 0)
    def _():
        acc_ref[...] = jnp.zeros_like(acc_ref)

    def cols(blk):
        return slice(blk * TB_PEER, (blk + 1) * TB_PEER)

    def scores(blk):
        z_ref[blk % 2] = jnp.dot(pltpu.bitcast(u_ref[...], BF16), pltpu.bitcast(ht_ref[:, cols(blk)], BF16),
                                 preferred_element_type=F32)

    def activations(blk):
        slot = blk % 2
        for k in range(groups):
            for half in range(TB_PEER // LANES):
                ls = slice(blk * TB_PEER + half * LANES, blk * TB_PEER + (half + 1) * LANES)
                zs = slice(half * LANES, (half + 1) * LANES)
                gate = [zero] * tiles
                for h in range(PEER_HEADS):
                    cnt = jnp.broadcast_to(cnt_ref[h, k:k + 1, ls], (SUB, LANES)).astype(BF16)
                    e1 = jnp.broadcast_to(e1_ref[h, k:k + 1, ls], (SUB, LANES)).astype(BF16)
                    for rt in range(tiles):
                        pr = slice(rt * SUB // 2, (rt + 1) * SUB // 2)
                        r2 = pltpu.bitcast(r2_ref[h, pr, ls], BF16)
                        e2 = pltpu.bitcast(e2_ref[h, pr, ls], BF16)
                        gate[rt] = gate[rt] + jnp.where(r2 < cnt, e2, zero) * e1
                for rt in range(tiles):
                    row0 = k * PEER_N_KEYS + rt * SUB
                    rr = slice(row0, row0 + SUB)
                    pr = slice(row0 // 2, (row0 + SUB) // 2)
                    a = _gelu_exact(z_ref[slot, rr, zs]).astype(BF16) * gate[rt]
                    a_ref[slot, pr, zs] = pltpu.bitcast(a, jnp.uint32)

    def project(blk):
        acc_ref[:, cols(blk)] += jnp.dot(pltpu.bitcast(vt_ref[...], BF16),
                                         pltpu.bitcast(a_ref[blk % 2], BF16),
                                         preferred_element_type=F32)

    scores(0)
    for blk in range(nblk):
        if blk + 1 < nblk:
            scores(blk + 1)
        activations(blk)
        project(blk)


def _peer(u, ht, vt, cnt, e1, r2, e2):
    T = ht.shape[1]
    tm, tn = TM_PEER, TN_PEER
    groups = tn // PEER_N_KEYS
    rows = pl.BlockSpec((PEER_HEADS, groups, tm), lambda i, j: (0, j, i))
    keys = pl.BlockSpec((PEER_HEADS, PEER_N_KEYS // 2, tm), lambda i, j: (0, 0, i))
    return pl.pallas_call(
        _peer_kernel,
        grid=(T // tm, PEER_N_EXPERTS // tn),
        in_specs=[pl.BlockSpec((tn // 2, D_MODEL), lambda i, j: (j, 0)),
                  pl.BlockSpec((D_MODEL // 2, tm), lambda i, j: (0, i)),
                  pl.BlockSpec((D_MODEL // 2, tn), lambda i, j: (0, j)),
                  rows, rows, keys, keys],
        out_specs=pl.BlockSpec((D_MODEL, tm), lambda i, j: (0, i)),
        out_shape=jax.ShapeDtypeStruct((D_MODEL, T), F32),
        scratch_shapes=[pltpu.VMEM((2, tn, TB_PEER), F32),
                        pltpu.VMEM((2, tn // 2, TB_PEER), jnp.uint32)],
        compiler_params=_cparams(("parallel", "arbitrary")),
        name="peer_ffn",
    )(u, ht, vt, cnt, e1, r2, e2)


def _ln2_kernel(h_ref, ft_ref, lnw_ref, lnb_ref, out_ref):
    y = ALPHA * h_ref[...] + ft_ref[...].T
    out_ref[...] = _layer_norm(y, lnw_ref[...], lnb_ref[...])


def _ln2(h, ffn_t, ln_w, ln_b):
    T = h.shape[0]
    tm = TM_OUT
    vec = pl.BlockSpec((1, D_MODEL), lambda i: (0, 0))
    return pl.pallas_call(
        _ln2_kernel,
        grid=(T // tm,),
        in_specs=[pl.BlockSpec((tm, D_MODEL), lambda i: (i, 0)),
                  pl.BlockSpec((D_MODEL, tm), lambda i: (0, i)), vec, vec],
        out_specs=pl.BlockSpec((tm, D_MODEL), lambda i: (i, 0)),
        out_shape=jax.ShapeDtypeStruct((T, D_MODEL), F32),
        compiler_params=_cparams(("parallel",)),
        name="residual_ln2",
    )(h, ffn_t, ln_w, ln_b)


def _pack_tables_kernel(u_ref, v_ref, up_ref, vtp_ref):
    up_ref[...] = pltpu.bitcast(u_ref[...].astype(BF16), jnp.uint32)
    vtp_ref[...] = pltpu.bitcast(v_ref[...].T.astype(BF16), jnp.uint32)


def _pack_tables(u, v):
    n = u.shape[0]
    tp = 512
    return pl.pallas_call(
        _pack_tables_kernel,
        grid=(n // tp,),
        in_specs=[pl.BlockSpec((tp, D_MODEL), lambda i: (i, 0)), pl.BlockSpec((tp, D_MODEL), lambda i: (i, 0))],
        out_specs=[pl.BlockSpec((tp // 2, D_MODEL), lambda i: (i, 0)),
                   pl.BlockSpec((D_MODEL // 2, tp), lambda i: (0, i))],
        out_shape=[jax.ShapeDtypeStruct((n // 2, D_MODEL), jnp.uint32),
                   jax.ShapeDtypeStruct((D_MODEL // 2, n), jnp.uint32)],
        compiler_params=_cparams(("parallel",)),
        name="pack_expert_tables",
    )(u, v)


def _rotary_tables(seq):
    pos = jnp.arange(seq, dtype=F32)

    def tab(d):
        inv = 1.0 / (ROPE_THETA ** (jnp.arange(0, d, 2, dtype=F32) / d))
        ang = pos[:, None] * inv[None, :]
        cos, sin = jnp.cos(ang), jnp.sin(ang)
        reps = LANES // d
        c = jnp.tile(jnp.concatenate([cos, cos], axis=-1), (1, reps))
        s = jnp.tile(jnp.concatenate([-sin, sin], axis=-1), (1, reps))
        return c, s

    cr, sr = tab(RET_HEAD_DIM)
    cs, ss = tab(SWA_HEAD_DIM)
    return cr, sr, cs, ss


def kernel(x, w_in, ret_gn_w, swa_sinks, w_out, ln1_w, ln1_b, w_pq, peer_sub_keys, peer_u, peer_v, ln2_w, ln2_b):
    B, S, D = x.shape
    assert D == D_MODEL and w_in.shape[0] == DEPTH
    T = B * S
    h = x.reshape(T, D)
    tabs = _rotary_tables(S)
    ret_consts = _retention_consts()
    for l in range(DEPTH):
        rq, rk, rv, rg, sq, kv8 = _inproj(h, w_in[l].astype(BF16), tabs, S)
        y_ret = _retention(rq, rk, rv, rg, ret_gn_w[l].reshape(1, RET_WIDTH), ret_consts, B, S)
        y_swa = _swa(sq, kv8, swa_sinks[l], S)
        keys = peer_sub_keys[l].reshape(2 * PEER_HEADS, PEER_N_KEYS, PEER_HALF).astype(BF16)
        h1, h1_t, scores_t = _outproj(y_ret, y_swa, h, w_out[l].astype(BF16), ln1_w[l].reshape(1, D),
                                      ln1_b[l].reshape(1, D), w_pq[l].astype(BF16), keys)
        cnt, e1, r2, e2 = _thresholds(scores_t)
        u_p, vt_p = _pack_tables(peer_u[l], peer_v[l])
        ffn_t = _peer(u_p, h1_t, vt_p, cnt, e1, r2, e2)
        h = _ln2(h1, ffn_t, ln2_w[l].reshape(1, D), ln2_b[l].reshape(1, D))
    return h.reshape(B, S, D)
```

```python
import functools
import math

import jax
import jax.numpy as jnp
from jax import lax
from jax.experimental import pallas as pl
from jax.experimental.pallas import tpu as pltpu

F32 = jnp.float32
BF16 = jnp.bfloat16

D_MODEL = 1024
RET_HEADS = 4
RET_HEAD_DIM = 128
RET_WIDTH = RET_HEADS * RET_HEAD_DIM
RET_CHUNK = 128
SWA_Q_HEADS = 8
SWA_KV_HEADS = 2
SWA_HEAD_DIM = 64
SWA_WIDTH = SWA_Q_HEADS * SWA_HEAD_DIM
SWA_KV_WIDTH = SWA_KV_HEADS * SWA_HEAD_DIM
WINDOW = 128
ROPE_THETA = 10000.0
PEER_HEADS = 8
PEER_N_KEYS = 128
PEER_N_EXPERTS = PEER_N_KEYS * PEER_N_KEYS
PEER_HALF = 128
PEER_TOPK = 16
LN_EPS = 1e-5
GN_EPS = 1e-6
DEPTH = 1
ALPHA = (2.0 * DEPTH) ** 0.25
NEG_INF = -1e30

LANES = 128
VMEM_LIMIT = 58 * 1024 * 1024

TM_IN = 512
TC_RET = 256
TQ_SWA = 512
TM_OUT = 512
TE_THR = 512
TM_PEER = 2048
TN_PEER = 1024
TB_PEER = 256


def _cparams(sem, **kw):
    return pltpu.CompilerParams(dimension_semantics=sem, vmem_limit_bytes=VMEM_LIMIT, **kw)


def _nt_dot(a, b):
    return lax.dot_general(a, b, (((1,), (1,)), ((), ())), preferred_element_type=F32)


def _inproj_kernel(x_ref, w_ref, cr_ref, sr_ref, cs_ref, ss_ref,
                   rq_ref, rk_ref, rv_ref, rg_ref, sq_ref, kv_ref):
    xb = x_ref[...].astype(BF16)
    cr, sr = cr_ref[...], sr_ref[...]
    cs, ss = cs_ref[...], ss_ref[...]

    def proj(c0, width):
        return jnp.dot(xb, w_ref[:, c0:c0 + width], preferred_element_type=F32)

    def rot_ret(p):
        return p * cr + pltpu.roll(p, 64, 1) * sr

    lane = lax.broadcasted_iota(jnp.int32, (x_ref.shape[0], LANES), 1)
    first_half = (lane % 64) < 32
    lo = lane < 64

    def rot_swa(p):
        swapped = jnp.where(first_half, pltpu.roll(p, 96, 1), pltpu.roll(p, 32, 1))
        return p * cs + swapped * ss

    pq = proj(0, RET_WIDTH)
    pk = proj(RET_WIDTH, RET_WIDTH)
    for h in range(RET_HEADS):
        c = slice(h * LANES, (h + 1) * LANES)
        rq_ref[:, c] = rot_ret(pq[:, c]).astype(BF16)
        rk_ref[:, c] = rot_ret(pk[:, c]).astype(BF16)
    rv_ref[...] = proj(2 * RET_WIDTH, RET_WIDTH).astype(BF16)
    rg_ref[...] = proj(3 * RET_WIDTH, RET_WIDTH)

    base = 4 * RET_WIDTH
    psq = proj(base, SWA_WIDTH)
    scale = SWA_HEAD_DIM ** -0.5
    for p in range(SWA_WIDTH // LANES):
        c = slice(p * LANES, (p + 1) * LANES)
        sq_ref[:, c] = (rot_swa(psq[:, c]) * scale).astype(BF16)

    pkv = proj(base + SWA_WIDTH, 2 * SWA_KV_WIDTH)
    sk = rot_swa(pkv[:, :LANES])
    sv = pkv[:, LANES:]
    zero = jnp.zeros_like(sk)
    for t, a in enumerate((sk, sv)):
        a_sw = pltpu.roll(a, 64, 1)
        o = t * 4 * LANES
        kv_ref[:, o + 0 * LANES:o + 1 * LANES] = jnp.where(lo, a, zero).astype(BF16)
        kv_ref[:, o + 1 * LANES:o + 2 * LANES] = jnp.where(lo, zero, a_sw).astype(BF16)
        kv_ref[:, o + 2 * LANES:o + 3 * LANES] = jnp.where(lo, a_sw, zero).astype(BF16)
        kv_ref[:, o + 3 * LANES:o + 4 * LANES] = jnp.where(lo, zero, a).astype(BF16)


def _inproj(xt, w_in, tabs, seq):
    T = xt.shape[0]
    tm = TM_IN
    nseq = seq // tm
    tab_spec = pl.BlockSpec((tm, LANES), lambda i: (i % nseq, 0))
    row = lambda w: pl.BlockSpec((tm, w), lambda i: (i, 0))
    return pl.pallas_call(
        _inproj_kernel,
        grid=(T // tm,),
        in_specs=[row(D_MODEL), pl.BlockSpec(w_in.shape, lambda i: (0, 0)),
                  tab_spec, tab_spec, tab_spec, tab_spec],
        out_specs=[row(RET_WIDTH), row(RET_WIDTH), row(RET_WIDTH), row(RET_WIDTH),
                   row(SWA_WIDTH), row(8 * LANES)],
        out_shape=[jax.ShapeDtypeStruct((T, RET_WIDTH), BF16),
                   jax.ShapeDtypeStruct((T, RET_WIDTH), BF16),
                   jax.ShapeDtypeStruct((T, RET_WIDTH), BF16),
                   jax.ShapeDtypeStruct((T, RET_WIDTH), F32),
                   jax.ShapeDtypeStruct((T, SWA_WIDTH), BF16),
                   jax.ShapeDtypeStruct((T, 8 * LANES), BF16)],
        compiler_params=_cparams(("parallel",)),
        name="inproj_rotary",
    )(xt, w_in, *tabs)


def _retention_kernel(rq_ref, rk_ref, rv_ref, rg_ref, gnw_ref, dec_ref, xi_ref, zeta_ref, g_ref,
                      y_ref, state_ref):
    @pl.when(pl.program_id(0) == 0)
    def _():
        state_ref[...] = jnp.zeros_like(state_ref)

    nb = rq_ref.shape[0]
    C = RET_CHUNK
    for b in range(nb):
        for h in range(RET_HEADS):
            cs = slice(h * LANES, (h + 1) * LANES)
            st = state_ref[b, h]
            for c in range(rq_ref.shape[1] // C):
                rs = slice(c * C, (c + 1) * C)
                q = rq_ref[b, rs, cs]
                k = rk_ref[b, rs, cs]
                v = rv_ref[b, rs, cs]
                s = _nt_dot(q, k) * dec_ref[h]
                y = jnp.dot(s.astype(BF16), v, preferred_element_type=F32)
                y = y + xi_ref[h] * jnp.dot(q, st.astype(BF16), preferred_element_type=F32)
                kz = (k.astype(F32) * zeta_ref[h]).T.astype(BF16)
                st = st * g_ref[h] + jnp.dot(kz, v, preferred_element_type=F32)
                mu = jnp.mean(y, axis=-1, keepdims=True)
                d = y - mu
                var = jnp.mean(d * d, axis=-1, keepdims=True)
                yn = d * lax.rsqrt(var + GN_EPS)
                gate = rg_ref[b, rs, cs]
                out = (gate * jax.nn.sigmoid(gate)) * (yn * gnw_ref[:, cs])
                y_ref[b, rs, cs] = out.astype(BF16)
            state_ref[b, h] = st


def _retention(rq, rk, rv, rg, gnw, consts, batch, seq):
    tc = TC_RET
    r3 = lambda a: a.reshape(batch, seq, RET_WIDTH)
    blk = pl.BlockSpec((batch, tc, RET_WIDTH), lambda j: (0, j, 0))
    cspec = pl.BlockSpec((RET_HEADS, RET_CHUNK, RET_CHUNK), lambda j: (0, 0, 0))
    y = pl.pallas_call(
        _retention_kernel,
        grid=(seq // tc,),
        in_specs=[blk, blk, blk, blk, pl.BlockSpec((1, RET_WIDTH), lambda j: (0, 0)),
                  cspec, cspec, cspec, cspec],
        out_specs=blk,
        out_shape=jax.ShapeDtypeStruct((batch, seq, RET_WIDTH), BF16),
        scratch_shapes=[pltpu.VMEM((batch, RET_HEADS, RET_HEAD_DIM, RET_HEAD_DIM), F32)],
        compiler_params=_cparams(("arbitrary",)),
        name="retention",
    )(r3(rq), r3(rk), r3(rv), r3(rg), gnw, *consts)
    return y.reshape(batch * seq, RET_WIDTH)


def _retention_consts():
    H, C, d = RET_HEADS, RET_CHUNK, RET_HEAD_DIM
    log_g = jnp.log(1.0 - 2.0 ** (-5.0 - jnp.arange(H, dtype=F32)))
    idx = jnp.arange(C, dtype=F32)
    diff = idx[:, None] - idx[None, :]
    scale = d ** -0.5
    decay = jnp.where(diff[None] >= 0, jnp.exp(jnp.maximum(diff, 0.0)[None] * log_g[:, None, None]), 0.0)
    zeta = jnp.exp((C - 1.0 - idx)[None] * log_g[:, None])
    xi = jnp.exp((idx + 1.0)[None] * log_g[:, None])
    g_chunk = jnp.exp(C * log_g)
    bc = lambda a: jnp.broadcast_to(a[:, :, None], (H, C, C)).astype(F32)
    return (decay * scale).astype(F32), bc(xi), bc(zeta * scale), \
        jnp.broadcast_to(g_chunk[:, None, None], (H, C, C)).astype(F32)


def _swa_kernel(sinks_ref, q_ref, kv_ref, kvp_ref, o_ref, *, blocks_per_seq):
    W = WINDOW
    i = pl.program_id(0)
    first = (i % blocks_per_seq) == 0
    qi = lax.broadcasted_iota(jnp.int32, (W, W), 0)
    kj = lax.broadcasted_iota(jnp.int32, (W, W), 1)
    cur_ok = kj <= qi
    prev_ok = kj > qi
    lane_lo = lax.broadcasted_iota(jnp.int32, (W, LANES), 1) < 64
    nq = q_ref.shape[0] // W
    for j in range(nq):
        rs = slice(j * W, (j + 1) * W)
        for p in range(SWA_WIDTH // LANES):
            kvh = p // 2
            q = q_ref[rs, p * LANES:(p + 1) * LANES]
            o_pair = None
            inv = []
            for e in range(2):
                kc = slice((kvh * 2 + e) * LANES, (kvh * 2 + e + 1) * LANES)
                vc = slice((4 + kvh * 2 + e) * LANES, (4 + kvh * 2 + e + 1) * LANES)
                if j == 0:
                    k_prev, v_prev = kvp_ref[:, kc], kvp_ref[:, vc]
                    p_ok = jnp.logical_and(prev_ok, jnp.logical_not(first))
                else:
                    ps = slice((j - 1) * W, j * W)
                    k_prev, v_prev = kv_ref[ps, kc], kv_ref[ps, vc]
                    p_ok = prev_ok
                sink = sinks_ref[2 * p + e]
                s_prev = jnp.where(p_ok, _nt_dot(q, k_prev), NEG_INF)
                s_cur = jnp.where(cur_ok, _nt_dot(q, kv_ref[rs, kc]), NEG_INF)
                m = jnp.maximum(jnp.maximum(jnp.max(s_prev, axis=-1, keepdims=True),
                                            jnp.max(s_cur, axis=-1, keepdims=True)), sink)
                p_prev = jnp.exp(s_prev - m)
                p_cur = jnp.exp(s_cur - m)
                denom = (jnp.sum(p_prev, axis=-1, keepdims=True) + jnp.sum(p_cur, axis=-1, keepdims=True)
                         + jnp.exp(sink - m))
                o = (jnp.dot(p_prev.astype(BF16), v_prev, preferred_element_type=F32)
                     + jnp.dot(p_cur.astype(BF16), kv_ref[rs, vc], preferred_element_type=F32))
                o_pair = o if o_pair is None else o_pair + o
                inv.append(1.0 / denom)
            o_ref[rs, p * LANES:(p + 1) * LANES] = (o_pair * jnp.where(lane_lo, inv[0], inv[1])).astype(BF16)


def _swa(sq, kv8, sinks, seq):
    T = sq.shape[0]
    tq = TQ_SWA
    nb = tq // WINDOW
    return pl.pallas_call(
        functools.partial(_swa_kernel, blocks_per_seq=seq // tq),
        grid=(T // tq,),
        in_specs=[pl.BlockSpec(memory_space=pltpu.SMEM),
                  pl.BlockSpec((tq, SWA_WIDTH), lambda i: (i, 0)),
                  pl.BlockSpec((tq, 8 * LANES), lambda i: (i, 0)),
                  pl.BlockSpec((WINDOW, 8 * LANES), lambda i: (jnp.maximum(i * nb - 1, 0), 0))],
        out_specs=pl.BlockSpec((tq, SWA_WIDTH), lambda i: (i, 0)),
        out_shape=jax.ShapeDtypeStruct((T, SWA_WIDTH), BF16),
        compiler_params=_cparams(("parallel",)),
        name="swa",
    )(sinks, sq, kv8, kv8)


def _layer_norm(y, w, b):
    mu = jnp.mean(y, axis=-1, keepdims=True)
    d = y - mu
    var = jnp.mean(d * d, axis=-1, keepdims=True)
    return d * lax.rsqrt(var + LN_EPS) * w + b


def _outproj_kernel(yr_ref, ys_ref, x_ref, wo_ref, lnw_ref, lnb_ref, wpq_ref, keys_ref,
                    h_ref, ht_ref, st_ref):
    mix = (jnp.dot(yr_ref[...], wo_ref[:RET_WIDTH, :], preferred_element_type=F32)
           + jnp.dot(ys_ref[...], wo_ref[RET_WIDTH:, :], preferred_element_type=F32))
    h = _layer_norm(ALPHA * x_ref[...] + mix, lnw_ref[...], lnb_ref[...])
    h_ref[...] = h
    ht_ref[...] = pltpu.bitcast(h.T.astype(BF16), jnp.uint32)
    hb = h.astype(BF16)
    for head in range(PEER_HEADS):
        c0 = head * 2 * PEER_HALF
        q = jnp.dot(hb, wpq_ref[:, c0:c0 + 2 * PEER_HALF], preferred_element_type=F32).astype(BF16)
        for p in range(2):
            st_ref[2 * head + p] = _nt_dot(keys_ref[2 * head + p], q[:, p * PEER_HALF:(p + 1) * PEER_HALF])


def _outproj(y_ret, y_swa, xt, w_out, ln_w, ln_b, w_pq, keys):
    T = xt.shape[0]
    tm = TM_OUT
    row = lambda w: pl.BlockSpec((tm, w), lambda i: (i, 0))
    full = lambda a: pl.BlockSpec(a.shape, lambda i: (0,) * a.ndim)
    return pl.pallas_call(
        _outproj_kernel,
        grid=(T // tm,),
        in_specs=[row(RET_WIDTH), row(SWA_WIDTH), row(D_MODEL), full(w_out), full(ln_w), full(ln_b),
                  full(w_pq), full(keys)],
        out_specs=[row(D_MODEL), pl.BlockSpec((D_MODEL // 2, tm), lambda i: (0, i)),
                   pl.BlockSpec((2 * PEER_HEADS, PEER_N_KEYS, tm), lambda i: (0, 0, i))],
        out_shape=[jax.ShapeDtypeStruct((T, D_MODEL), F32),
                   jax.ShapeDtypeStruct((D_MODEL // 2, T), jnp.uint32),
                   jax.ShapeDtypeStruct((2 * PEER_HEADS, PEER_N_KEYS, T), F32)],
        compiler_params=_cparams(("parallel",)),
        name="outproj_ln1_scores",
    )(y_ret, y_swa, xt, w_out, ln_w, ln_b, w_pq, keys)


def _odd_even_merge_sort_pairs(n):
    pairs = []
    p = 1
    while p < n:
        k = p
        while k >= 1:
            for j in range(k % p, n - k, 2 * k):
                for i in range(min(k, n - j - k)):
                    if (i + j) // (2 * p) == (i + j + k) // (2 * p):
                        pairs.append((i + j, i + j + k))
            k //= 2
        p *= 2
    return pairs


_SORT16 = _odd_even_merge_sort_pairs(16)


def _top_rows(s, k, want_rank):
    sub = 8
    n = s.shape[0] // sub
    v = [s[i * sub:(i + 1) * sub] for i in range(n)]
    for i, j in _SORT16:
        v[i], v[j] = jnp.maximum(v[i], v[j]), jnp.minimum(v[i], v[j])
    rows = []
    for a in range(k):
        m = jnp.max(v[0], axis=0, keepdims=True)
        rows.append(m)
        hit = v[0] == m
        for i in range(k - 1 - a):
            v[i] = jnp.where(hit, v[i + 1], v[i])
    if not want_rank:
        return rows, None
    rank = jnp.zeros(s.shape, F32)
    for a in range(k):
        rank = jnp.where(rows[a] > s, float(a + 1), rank)
    return rows, rank


def _stack_rows(rows, lo, hi):
    sub = lax.broadcasted_iota(jnp.int32, (8, rows[0].shape[1]), 0)
    out = jnp.full((8, rows[0].shape[1]), -jnp.inf, F32)
    for r in range(lo, hi):
        out = jnp.where(sub == (r - lo), rows[r], out)
    return out


def _threshold_kernel(s_ref, cnt_ref, e1_ref, r2_ref, e2_ref):
    K = PEER_TOPK
    L = LANES

    @pl.loop(0, s_ref.shape[2] // L)
    def _(blk):
        ls = pl.ds(pl.multiple_of(blk * L, L), L)
        for h in range(PEER_HEADS):
            s1 = s_ref[2 * h, :, ls]
            s2 = s_ref[2 * h + 1, :, ls]
            t1, _ = _top_rows(s1, K, False)
            t2, r2 = _top_rows(s2, K, True)
            t1_lo, t1_hi = _stack_rows(t1, 0, 8), _stack_rows(t1, 8, 16)
            t2_hi = _stack_rows(t2, 8, 16)
            sub = lax.broadcasted_iota(jnp.int32, (8, L), 0)
            cands = [t1_lo + t2[0], t1_hi + t2[0], t2_hi + t1[0]]
            for b in range(1, 8):
                a_max = K // (b + 1) - 1
                cands.append(jnp.where(sub <= a_max, t1_lo + t2[b], -jnp.inf))
            w = list(cands)
            tau = None
            for a in range(K):
                m = w[0]
                for c in w[1:]:
                    m = jnp.maximum(m, c)
                m = jnp.max(m, axis=0, keepdims=True)
                tau = m
                if a + 1 < K:
                    w = [jnp.where(c == m, -jnp.inf, c) for c in w]
            top = t1[0] + t2[0]
            z = None
            for c in cands:
                e = jnp.where(c >= tau, jnp.exp(c - top), 0.0)
                z = e if z is None else z + e
            inv_z = 1.0 / jnp.sum(z, axis=0, keepdims=True)
            cnt = jnp.zeros(s1.shape, F32)
            for b in range(8):
                cnt = jnp.where(s1 + t2[b] >= tau, float(b + 1), cnt)
            n_hi = jnp.sum(jnp.where(t1[0] + t2_hi >= tau, 1.0, 0.0), axis=0, keepdims=True)
            cnt_ref[h, :, ls] = cnt + jnp.where(s1 == t1[0], n_hi, 0.0)
            e1_ref[h, :, ls] = jnp.exp(s1 - t1[0]) * inv_z
            r2_ref[h, :, ls] = pltpu.bitcast(r2.astype(BF16), jnp.uint32)
            e2_ref[h, :, ls] = pltpu.bitcast(jnp.exp(s2 - t2[0]).astype(BF16), jnp.uint32)


def _thresholds(scores_t):
    T = scores_t.shape[2]
    te = TE_THR
    blk = pl.BlockSpec((PEER_HEADS, PEER_N_KEYS, te), lambda i: (0, 0, i))
    shp = jax.ShapeDtypeStruct((PEER_HEADS, PEER_N_KEYS, T), F32)
    pblk = pl.BlockSpec((PEER_HEADS, PEER_N_KEYS // 2, te), lambda i: (0, 0, i))
    pshp = jax.ShapeDtypeStruct((PEER_HEADS, PEER_N_KEYS // 2, T), jnp.uint32)
    return pl.pallas_call(
        _threshold_kernel,
        grid=(T // te,),
        in_specs=[pl.BlockSpec((2 * PEER_HEADS, PEER_N_KEYS, te), lambda i: (0, 0, i))],
        out_specs=[blk, blk, pblk, pblk],
        out_shape=[shp, shp, pshp, pshp],
        compiler_params=_cparams(("parallel",)),
        name="peer_threshold",
    )(scores_t)


def _gelu_exact(z):
    return 0.5 * z * (1.0 + lax.erf(z * (2.0 ** -0.5)))


def _peer_kernel(u_ref, ht_ref, vt_ref, cnt_ref, e1_ref, r2_ref, e2_ref, acc_ref, z_ref, a_ref):
    j = pl.program_id(1)
    tn = 2 * u_ref.shape[0]
    tm = ht_ref.shape[1]
    groups = tn // PEER_N_KEYS
    nblk = tm // TB_PEER
    SUB = 16
    tiles = PEER_N_KEYS // SUB
    zero = jnp.zeros((SUB, LANES), BF16)

    @pl.when(j == 0)
    def _():
        acc_ref[...] = jnp.zeros_like(acc_ref)

    def cols(blk):
        return slice(blk * TB_PEER, (blk + 1) * TB_PEER)

    def scores(blk):
        z_ref[blk % 2] = jnp.dot(pltpu.bitcast(u_ref[...], BF16), pltpu.bitcast(ht_ref[:, cols(blk)], BF16),
                                 preferred_element_type=F32)

    def activations(blk):
        slot = blk % 2
        for k in range(groups):
            for half in range(TB_PEER // LANES):
                ls = slice(blk * TB_PEER + half * LANES, blk * TB_PEER + (half + 1) * LANES)
                zs = slice(half * LANES, (half + 1) * LANES)
                gate = [zero] * tiles
                for h in range(PEER_HEADS):
                    cnt = jnp.broadcast_to(cnt_ref[h, k:k + 1, ls], (SUB, LANES)).astype(BF16)
                    e1 = jnp.broadcast_to(e1_ref[h, k:k + 1, ls], (SUB, LANES)).astype(BF16)
                    for rt in range(tiles):
                        pr = slice(rt * SUB // 2, (rt + 1) * SUB // 2)
                        r2 = pltpu.bitcast(r2_ref[h, pr, ls], BF16)
                        e2 = pltpu.bitcast(e2_ref[h, pr, ls], BF16)
                        gate[rt] = gate[rt] + jnp.where(r2 < cnt, e2, zero) * e1
                for rt in range(tiles):
                    row0 = k * PEER_N_KEYS + rt * SUB
                    rr = slice(row0, row0 + SUB)
                    pr = slice(row0 // 2, (row0 + SUB) // 2)
                    a = _gelu_exact(z_ref[slot, rr, zs]).astype(BF16) * gate[rt]
                    a_ref[slot, pr, zs] = pltpu.bitcast(a, jnp.uint32)

    def project(blk):
        acc_ref[:, cols(blk)] += jnp.dot(pltpu.bitcast(vt_ref[...], BF16),
                                         pltpu.bitcast(a_ref[blk % 2], BF16),
                                         preferred_element_type=F32)

    scores(0)
    for blk in range(nblk):
        if blk + 1 < nblk:
            scores(blk + 1)
        activations(blk)
        project(blk)


def _peer(u, ht, vt, cnt, e1, r2, e2):
    T = ht.shape[1]
    tm, tn = TM_PEER, TN_PEER
    groups = tn // PEER_N_KEYS
    rows = pl.BlockSpec((PEER_HEADS, groups, tm), lambda i, j: (0, j, i))
    keys = pl.BlockSpec((PEER_HEADS, PEER_N_KEYS // 2, tm), lambda i, j: (0, 0, i))
    return pl.pallas_call(
        _peer_kernel,
        grid=(T // tm, PEER_N_EXPERTS // tn),
        in_specs=[pl.BlockSpec((tn // 2, D_MODEL), lambda i, j: (j, 0)),
                  pl.BlockSpec((D_MODEL // 2, tm), lambda i, j: (0, i)),
                  pl.BlockSpec((D_MODEL // 2, tn), lambda i, j: (0, j)),
                  rows, rows, keys, keys],
        out_specs=pl.BlockSpec((D_MODEL, tm), lambda i, j: (0, i)),
        out_shape=jax.ShapeDtypeStruct((D_MODEL, T), F32),
        scratch_shapes=[pltpu.VMEM((2, tn, TB_PEER), F32),
                        pltpu.VMEM((2, tn // 2, TB_PEER), jnp.uint32)],
        compiler_params=_cparams(("parallel", "arbitrary")),
        name="peer_ffn",
    )(u, ht, vt, cnt, e1, r2, e2)


def _ln2_kernel(h_ref, ft_ref, lnw_ref, lnb_ref, out_ref):
    y = ALPHA * h_ref[...] + ft_ref[...].T
    out_ref[...] = _layer_norm(y, lnw_ref[...], lnb_ref[...])


def _ln2(h, ffn_t, ln_w, ln_b):
    T = h.shape[0]
    tm = TM_OUT
    vec = pl.BlockSpec((1, D_MODEL), lambda i: (0, 0))
    return pl.pallas_call(
        _ln2_kernel,
        grid=(T // tm,),
        in_specs=[pl.BlockSpec((tm, D_MODEL), lambda i: (i, 0)),
                  pl.BlockSpec((D_MODEL, tm), lambda i: (0, i)), vec, vec],
        out_specs=pl.BlockSpec((tm, D_MODEL), lambda i: (i, 0)),
        out_shape=jax.ShapeDtypeStruct((T, D_MODEL), F32),
        compiler_params=_cparams(("parallel",)),
        name="residual_ln2",
    )(h, ffn_t, ln_w, ln_b)


def _pack_tables_kernel(u_ref, v_ref, up_ref, vtp_ref):
    up_ref[...] = pltpu.bitcast(u_ref[...].astype(BF16), jnp.uint32)
    vtp_ref[...] = pltpu.bitcast(v_ref[...].T.astype(BF16), jnp.uint32)


def _pack_tables(u, v):
    n = u.shape[0]
    tp = 512
    return pl.pallas_call(
        _pack_tables_kernel,
        grid=(n // tp,),
        in_specs=[pl.BlockSpec((tp, D_MODEL), lambda i: (i, 0)), pl.BlockSpec((tp, D_MODEL), lambda i: (i, 0))],
        out_specs=[pl.BlockSpec((tp // 2, D_MODEL), lambda i: (i, 0)),
                   pl.BlockSpec((D_MODEL // 2, tp), lambda i: (0, i))],
        out_shape=[jax.ShapeDtypeStruct((n // 2, D_MODEL), jnp.uint32),
                   jax.ShapeDtypeStruct((D_MODEL // 2, n), jnp.uint32)],
        compiler_params=_cparams(("parallel",)),
        name="pack_expert_tables",
    )(u, v)


def _rotary_tables(seq):
    pos = jnp.arange(seq, dtype=F32)

    def tab(d):
        inv = 1.0 / (ROPE_THETA ** (jnp.arange(0, d, 2, dtype=F32) / d))
        ang = pos[:, None] * inv[None, :]
        cos, sin = jnp.cos(ang), jnp.sin(ang)
        reps = LANES // d
        c = jnp.tile(jnp.concatenate([cos, cos], axis=-1), (1, reps))
        s = jnp.tile(jnp.concatenate([-sin, sin], axis=-1), (1, reps))
        return c, s

    cr, sr = tab(RET_HEAD_DIM)
    cs, ss = tab(SWA_HEAD_DIM)
    return cr, sr, cs, ss


def kernel(x, w_in, ret_gn_w, swa_sinks, w_out, ln1_w, ln1_b, w_pq, peer_sub_keys, peer_u, peer_v, ln2_w, ln2_b):
    B, S, D = x.shape
    assert D == D_MODEL and w_in.shape[0] == DEPTH
    T = B * S
    h = x.reshape(T, D)
    tabs = _rotary_tables(S)
    ret_consts = _retention_consts()
    for l in range(DEPTH):
        rq, rk, rv, rg, sq, kv8 = _inproj(h, w_in[l].astype(BF16), tabs, S)
        y_ret = _retention(rq, rk, rv, rg, ret_gn_w[l].reshape(1, RET_WIDTH), ret_consts, B, S)
        y_swa = _swa(sq, kv8, swa_sinks[l], S)
        keys = peer_sub_keys[l].reshape(2 * PEER_HEADS, PEER_N_KEYS, PEER_HALF).astype(BF16)
        h1, h1_t, scores_t = _outproj(y_ret, y_swa, h, w_out[l].astype(BF16), ln1_w[l].reshape(1, D),
                                      ln1_b[l].reshape(1, D), w_pq[l].astype(BF16), keys)
        cnt, e1, r2, e2 = _thresholds(scores_t)
        u_p, vt_p = _pack_tables(peer_u[l], peer_v[l])
        ffn_t = _peer(u_p, h1_t, vt_p, cnt, e1, r2, e2)
        h = _ln2(h1, ffn_t, ln2_w[l].reshape(1, D), ln2_b[l].reshape(1, D))
    return h.reshape(B, S, D)
```
